```python
import math
import jax, jax.numpy as jnp
from jax import lax
import numpy as np

D_MODEL = 1024
BATCH = 4
SEQ = 8192
DEPTH = 2

ROPE_THETA = 500000.0
Q_BLOCK = 128
LN_EPS = 1e-5
POOL_WIDTH = 512
POOL_WINDOWS = (2, 4, 8, 16)
POOL_GROUP = POOL_WIDTH // 4
CONV_WIDTH = 512
CONV_K = 31
DSA_HEADS = 8
DSA_HEAD_DIM = 64
DSA_ROT = DSA_HEAD_DIM // 4
IDX_HEADS = 8
IDX_DIM = 32
IDX_ROT = IDX_DIM // 4
DSA_TOPK = 256
MLA_HEADS = 8
MLA_NOPE = 64
MLA_ROPE = 32
MLA_V = 64
MLA_Q_RANK = 384
MLA_KV_RANK = 256
N_BRANCH = 4
N_EXPERTS = 32
TOP_K = 4
D_FF = 1024
SWIGLU_ALPHA = 1.702
SWIGLU_LIMIT = 7.0
MOE_BLOCK = 512
DEEPNORM_ALPHA = (2 * DEPTH) ** 0.25
DEEPNORM_BETA = (8 * DEPTH) ** -0.25
SPLITS = (POOL_WIDTH, 2 * CONV_WIDTH,
          DSA_HEADS * DSA_HEAD_DIM, DSA_HEADS * DSA_HEAD_DIM, DSA_HEADS * DSA_HEAD_DIM,
          IDX_HEADS * IDX_DIM, IDX_DIM, IDX_HEADS,
          MLA_Q_RANK, MLA_KV_RANK, MLA_ROPE,
          N_BRANCH * D_MODEL)
D_IN = sum(SPLITS)

kernel_name = 'hybrid_pool_conv_dsa_mla_moe'

F32 = jnp.float32


def _layernorm(x, g, b):
    xf = x.astype(F32)
    mu = jnp.mean(xf, axis=-1, keepdims=True)
    var = jnp.mean(jnp.square(xf - mu), axis=-1, keepdims=True)
    y = (xf - mu) * lax.rsqrt(var + LN_EPS) * g.astype(F32) + b.astype(F32)
    return y.astype(x.dtype)


def _rmsnorm(x, g):
    xf = x.astype(F32)
    y = xf * lax.rsqrt(jnp.mean(jnp.square(xf), axis=-1, keepdims=True) + LN_EPS) * g.astype(F32)
    return y.astype(x.dtype)


def _rope_tables(positions, rot):
    inv = jnp.power(ROPE_THETA, -jnp.arange(0, rot, 2, dtype=F32) / rot)
    ang = positions.astype(F32)[..., None] * inv
    return jnp.cos(ang)[:, :, None, :], jnp.sin(ang)[:, :, None, :]


def _apply_rope(x, cos, sin, rot):
    xr = x[..., :rot].astype(F32)
    x1, x2 = xr[..., :rot // 2], xr[..., rot // 2:]
    r = jnp.concatenate([x1 * cos - x2 * sin, x2 * cos + x1 * sin], axis=-1)
    return jnp.concatenate([r.astype(x.dtype), x[..., rot:]], axis=-1)


def _pool_mixer(u, pool_w, pool_scale, pool_out):
    B, S, _ = u.shape
    ug = u.astype(F32).reshape(B, S, len(POOL_WINDOWS), POOL_GROUP)
    cs = jnp.cumsum(ug, axis=1)
    pos = jnp.arange(S)
    outs = []
    for gi, w in enumerate(POOL_WINDOWS):
        c = cs[:, :, gi]
        lag = jnp.pad(c[:, :S - w], ((0, 0), (w, 0), (0, 0)))
        mean = (c - lag) / jnp.minimum(pos + 1, w).astype(F32)[None, :, None]
        outs.append(mean - ug[:, :, gi])
    pooled = jnp.stack(outs, axis=2).astype(u.dtype)
    mixed = jnp.einsum('bsgc,gcd->bsgd', pooled, pool_w).reshape(B, S, POOL_WIDTH) * pool_scale
    return mixed @ pool_out


def _conv_mixer(u, conv_w, conv_b, ln_g, ln_b, conv_out):
    a, gate = jnp.split(u, 2, axis=-1)
    h = a * jax.nn.sigmoid(gate)
    h = lax.conv_general_dilated(h, conv_w[:, None, :], window_strides=(1,),
                                 padding=[(CONV_K - 1, 0)],
                                 dimension_numbers=('NWC', 'WIO', 'NWC'),
                                 feature_group_count=CONV_WIDTH) + conv_b
    h = jax.nn.silu(_layernorm(h, ln_g, ln_b))
    return h @ conv_out


def _dsa_mixer(q, k, v, qi, ki, wi, rope_c, rope_i, dsa_out):
    B, S, _ = q.shape
    q = _apply_rope(q.reshape(B, S, DSA_HEADS, DSA_HEAD_DIM), *rope_c, DSA_ROT)
    k = _apply_rope(k.reshape(B, S, DSA_HEADS, DSA_HEAD_DIM), *rope_c, DSA_ROT)
    v = v.reshape(B, S, DSA_HEADS, DSA_HEAD_DIM)
    qi = _apply_rope(qi.reshape(B, S, IDX_HEADS, IDX_DIM), *rope_i, IDX_ROT)
    ki = _apply_rope(ki[:, :, None, :], *rope_i, IDX_ROT)[:, :, 0].astype(F32)
    wi = wi.astype(F32) * IDX_HEADS ** -0.5
    n_sel = min(DSA_TOPK, S // 4)
    nb = S // Q_BLOCK
    key_pos = jnp.arange(S)

    def to_blocks(t):
        return jnp.moveaxis(t.reshape(B, nb, Q_BLOCK, *t.shape[2:]), 1, 0)

    def block(args):
        qb, qib, wb, i = args
        qpos = i * Q_BLOCK + jnp.arange(Q_BLOCK)
        logits = jnp.einsum('bqhd,bsd->bqhs', qib.astype(F32), ki) * IDX_DIM ** -0.5
        score = jnp.einsum('bqh,bqhs->bqs', wb, jax.nn.relu(logits))
        causal = key_pos[None, :] <= qpos[:, None]
        score = jnp.where(causal[None], score, -jnp.inf)
        _, sel = lax.top_k(score, n_sel)
        valid = sel <= qpos[None, :, None]
        k_sel = jax.vmap(lambda kb, ib: kb[ib])(k, sel)
        v_sel = jax.vmap(lambda vb, ib: vb[ib])(v, sel)
        s = jnp.einsum('bqhd,bqnhd->bqhn', qb, k_sel).astype(F32) * DSA_HEAD_DIM ** -0.5
        s = jnp.where(valid[:, :, None, :], s, -jnp.inf)
        p = jax.nn.softmax(s, axis=-1).astype(v.dtype)
        return jnp.einsum('bqhn,bqnhd->bqhd', p, v_sel)

    o = lax.map(block, (to_blocks(q), to_blocks(qi), to_blocks(wi), jnp.arange(nb)))
    o = jnp.moveaxis(o, 0, 1).reshape(B, S, DSA_HEADS * DSA_HEAD_DIM)
    return o @ dsa_out


def _mla_mixer(cq, ckv, kr, q_norm, kv_norm, wuq, wuk, wuv, rope_m, mla_out):
    B, S, _ = cq.shape
    q = (_rmsnorm(cq, q_norm) @ wuq).reshape(B, S, MLA_HEADS, MLA_NOPE + MLA_ROPE)
    q_nope = q[..., :MLA_NOPE]
    q_rope = _apply_rope(q[..., MLA_NOPE:], *rope_m, MLA_ROPE)
    c = _rmsnorm(ckv, kv_norm)
    k_nope = (c @ wuk).reshape(B, S, MLA_HEADS, MLA_NOPE)
    v = (c @ wuv).reshape(B, S, MLA_HEADS, MLA_V)
    k_rope = _apply_rope(kr[:, :, None, :], *rope_m, MLA_ROPE)[:, :, 0]
    scale = (MLA_NOPE + MLA_ROPE) ** -0.5
    outs = []
    for i in range(S // Q_BLOCK):
        lo, hi = i * Q_BLOCK, (i + 1) * Q_BLOCK
        s = (jnp.einsum('bqhd,bkhd->bhqk', q_nope[:, lo:hi], k_nope[:, :hi])
             + jnp.einsum('bqhr,bkr->bhqk', q_rope[:, lo:hi], k_rope[:, :hi]))
        mask = jnp.arange(hi)[None, :] <= (lo + jnp.arange(Q_BLOCK))[:, None]
        s = jnp.where(mask, s.astype(F32) * scale, -jnp.inf)
        p = jax.nn.softmax(s, axis=-1).astype(v.dtype)
        outs.append(jnp.einsum('bhqk,bkhd->bqhd', p, v[:, :hi]))
    o = jnp.concatenate(outs, axis=1).reshape(B, S, MLA_HEADS * MLA_V)
    return o @ mla_out


def _mixing_block(x, ropes, w_in, pool_w, pool_scale, pool_out, conv_w, conv_b, conv_ln_g, conv_ln_b,
                  conv_out, dsa_out, mla_q_norm, mla_kv_norm, mla_wuq, mla_wuk, mla_wuv, mla_out, w_o):
    B, S, D = x.shape
    proj = x @ w_in
    offsets = [int(o) for o in np.cumsum(SPLITS)[:-1]]
    (u_pool, u_conv, c_q, c_k, c_v, i_q, i_k, i_w, m_q, m_kv, m_kr, g_all) = jnp.split(proj, offsets, axis=-1)
    rope_c, rope_i, rope_m = ropes
    y_a = _pool_mixer(u_pool, pool_w, pool_scale, pool_out)
    y_b = _conv_mixer(u_conv, conv_w, conv_b, conv_ln_g, conv_ln_b, conv_out)
    y_c = _dsa_mixer(c_q, c_k, c_v, i_q, i_k, i_w, rope_c, rope_i, dsa_out)
    y_d = _mla_mixer(m_q, m_kv, m_kr, mla_q_norm, mla_kv_norm, mla_wuq, mla_wuk, mla_wuv, rope_m, mla_out)
    gates = jax.nn.sigmoid(g_all.reshape(B, S, N_BRANCH, D))
    merged = gates[:, :, 0] * y_a + gates[:, :, 1] * y_b + gates[:, :, 2] * y_c + gates[:, :, 3] * y_d
    return merged @ w_o


def _moe(h, router_w, router_b, w_gu, b_gu, w_d, b_d):
    B, S, D = h.shape
    n_tok = B * S
    xt = h.reshape(n_tok, D)
    logits = (xt @ router_w + router_b).astype(F32)
    top_val, top_idx = lax.top_k(logits, TOP_K)
    gate = jax.nn.softmax(top_val, axis=-1)
    n_assign = n_tok * TOP_K
    flat_e = top_idx.reshape(n_assign)
    order = jnp.argsort(flat_e)
    e_sorted = flat_e[order]
    tok_sorted = order // TOP_K
    gate_sorted = gate.reshape(n_assign)[order].astype(h.dtype)
    counts = jnp.bincount(flat_e, length=N_EXPERTS)
    padded = (counts + MOE_BLOCK - 1) // MOE_BLOCK * MOE_BLOCK
    pad_end = jnp.cumsum(padded)
    pad_start = pad_end - padded
    grp_start = jnp.cumsum(counts) - counts
    dest = pad_start[e_sorted] + (jnp.arange(n_assign) - grp_start[e_sorted])
    n_blocks = -(-n_assign // MOE_BLOCK) + N_EXPERTS
    buf = jnp.zeros((n_blocks * MOE_BLOCK, D), h.dtype).at[dest].set(xt[tok_sorted])
    blk_expert = jnp.minimum(jnp.searchsorted(pad_end, jnp.arange(n_blocks) * MOE_BLOCK, side='right'),
                             N_EXPERTS - 1)

    def expert_block(args):
        xb, e = args
        gu = xb @ w_gu[e] + b_gu[e]
        g = jnp.minimum(gu[:, :D_FF], SWIGLU_LIMIT)
        lin = jnp.clip(gu[:, D_FF:], -SWIGLU_LIMIT, SWIGLU_LIMIT)
        act = g * jax.nn.sigmoid(SWIGLU_ALPHA * g) * (lin + 1)
        return act @ w_d[e] + b_d[e]

    y_buf = lax.map(expert_block, (buf.reshape(n_blocks, MOE_BLOCK, D), blk_expert)).reshape(-1, D)
    y = y_buf[dest] * gate_sorted[:, None]
    out = jnp.zeros((n_tok, D), h.dtype).at[tok_sorted].add(y)
    return out.reshape(B, S, D)


def setup_inputs(seed: int = 0) -> dict:
    key = jax.random.key(seed)
    ks = iter(jax.random.split(key, 40))
    L, D, E = DEPTH, D_MODEL, N_EXPERTS

    def nrm(shape, scale):
        return jax.random.normal(next(ks), shape, F32) * scale

    def gain(shape):
        return 1.0 + nrm(shape, 0.02)

    x = nrm((BATCH, SEQ, D), 1.0)
    offset = jax.random.randint(next(ks), (BATCH, 1), 0, 4096, dtype=jnp.int32)
    positions = offset + jnp.arange(SEQ, dtype=jnp.int32)[None, :]
    return {
        'x': x,
        'positions': positions,
        'w_in': nrm((L, D, D_IN), D ** -0.5),
        'pool_w': nrm((L, 4, POOL_GROUP, POOL_GROUP), POOL_GROUP ** -0.5),
        'pool_scale': gain((L, POOL_WIDTH)),
        'pool_out': nrm((L, POOL_WIDTH, D), POOL_WIDTH ** -0.5),
        'conv_w': nrm((L, CONV_K, CONV_WIDTH), CONV_K ** -0.5),
        'conv_b': nrm((L, CONV_WIDTH), 0.02),
        'conv_ln_g': gain((L, CONV_WIDTH)),
        'conv_ln_b': nrm((L, CONV_WIDTH), 0.02),
        'conv_out': nrm((L, CONV_WIDTH, D), CONV_WIDTH ** -0.5),
        'dsa_out': nrm((L, DSA_HEADS * DSA_HEAD_DIM, D), (DSA_HEADS * DSA_HEAD_DIM) ** -0.5),
        'mla_q_norm': gain((L, MLA_Q_RANK)),
        'mla_kv_norm': gain((L, MLA_KV_RANK)),
        'mla_wuq': nrm((L, MLA_Q_RANK, MLA_HEADS * (MLA_NOPE + MLA_ROPE)), MLA_Q_RANK ** -0.5),
        'mla_wuk': nrm((L, MLA_KV_RANK, MLA_HEADS * MLA_NOPE), MLA_KV_RANK ** -0.5),
        'mla_wuv': nrm((L, MLA_KV_RANK, MLA_HEADS * MLA_V), MLA_KV_RANK ** -0.5),
        'mla_out': nrm((L, MLA_HEADS * MLA_V, D), (MLA_HEADS * MLA_V) ** -0.5),
        'w_o': nrm((L, D, D), D ** -0.5 * DEEPNORM_BETA),
        'ln1_g': gain((L, D)),
        'ln1_b': nrm((L, D), 0.02),
        'router_w': nrm((L, D, E), D ** -0.5),
        'router_b': nrm((L, E), 0.01),
        'exp_w_gu': nrm((L, E, D, 2 * D_FF), D ** -0.5),
        'exp_b_gu': nrm((L, E, 2 * D_FF), 0.02),
        'exp_w_d': nrm((L, E, D_FF, D), D_FF ** -0.5 * DEEPNORM_BETA),
        'exp_b_d': nrm((L, E, D), 0.02),
        'ln2_g': gain((L, D)),
        'ln2_b': nrm((L, D), 0.02),
    }


def reference(x, positions, w_in, pool_w, pool_scale, pool_out, conv_w, conv_b, conv_ln_g, conv_ln_b,
              conv_out, dsa_out, mla_q_norm, mla_kv_norm, mla_wuq, mla_wuk, mla_wuv, mla_out, w_o,
              ln1_g, ln1_b, router_w, router_b, exp_w_gu, exp_b_gu, exp_w_d, exp_b_d, ln2_g, ln2_b):
    ropes = (_rope_tables(positions, DSA_ROT), _rope_tables(positions, IDX_ROT),
             _rope_tables(positions, MLA_ROPE))
    for l in range(DEPTH):
        y = _mixing_block(x, ropes, w_in[l], pool_w[l], pool_scale[l], pool_out[l], conv_w[l], conv_b[l],
                          conv_ln_g[l], conv_ln_b[l], conv_out[l], dsa_out[l], mla_q_norm[l],
                          mla_kv_norm[l], mla_wuq[l], mla_wuk[l], mla_wuv[l], mla_out[l], w_o[l])
        x = _layernorm(DEEPNORM_ALPHA * x + y, ln1_g[l], ln1_b[l])
        y = _moe(x, router_w[l], router_b[l], exp_w_gu[l], exp_b_gu[l], exp_w_d[l], exp_b_d[l])
        x = _layernorm(DEEPNORM_ALPHA * x + y, ln2_g[l], ln2_b[l])
    return x
```

```python
import functools

import numpy as np
import jax
import jax.numpy as jnp
from jax import lax
from jax.experimental import pallas as pl
from jax.experimental.pallas import tpu as pltpu

F32 = jnp.float32
BF16 = jnp.bfloat16
I32 = jnp.int32

D_MODEL = 1024
DEPTH = 2
ROPE_THETA = 500000.0
LN_EPS = 1e-5
POOL_WIDTH = 512
POOL_WINDOWS = (2, 4, 8, 16)
POOL_GROUP = POOL_WIDTH // 4
CONV_WIDTH = 512
CONV_K = 31
DSA_HEADS = 8
DSA_HEAD_DIM = 64
DSA_ROT = DSA_HEAD_DIM // 4
IDX_HEADS = 8
IDX_DIM = 32
IDX_ROT = IDX_DIM // 4
DSA_TOPK = 256
MLA_HEADS = 8
MLA_NOPE = 64
MLA_ROPE = 32
MLA_V = 64
MLA_Q_RANK = 384
MLA_KV_RANK = 256
N_BRANCH = 4
N_EXPERTS = 32
TOP_K = 4
D_FF = 1024
SWIGLU_ALPHA = 1.702
SWIGLU_LIMIT = 7.0
MOE_BLOCK = 512
DEEPNORM_ALPHA = (2 * DEPTH) ** 0.25
SPLITS = (POOL_WIDTH, 2 * CONV_WIDTH,
          DSA_HEADS * DSA_HEAD_DIM, DSA_HEADS * DSA_HEAD_DIM, DSA_HEADS * DSA_HEAD_DIM,
          IDX_HEADS * IDX_DIM, IDX_DIM, IDX_HEADS,
          MLA_Q_RANK, MLA_KV_RANK, MLA_ROPE,
          N_BRANCH * D_MODEL)

LANES = 128
HALO = 32
VMEM_LIMIT = 56 * 1024 * 1024

COL_GATE = 0
COL_CONV = 4096
COL_POOL = 5120
COL_CQ = 5632
COL_CK = 6144
COL_CV = 6656
COL_MQ = 7168
COL_MKV = 7680
COL_IQ = 7936
COL_IKW = 8192
COL_MKR = 8320
P_COLS = 8448

INT_MIN = -(2 ** 31)
NEG_BIG = -1e30


def _cparams(sem, vmem=VMEM_LIMIT):
    return pltpu.CompilerParams(dimension_semantics=sem, vmem_limit_bytes=vmem)


def _dot(a, b):
    return jnp.dot(a, b, preferred_element_type=F32)


def _dot_nt(a, b):
    return lax.dot_general(a, b, (((1,), (1,)), ((), ())), preferred_element_type=F32)


def _layernorm(z, g, b):
    mu = jnp.mean(z, axis=-1, keepdims=True)
    zc = z - mu
    var = jnp.mean(zc * zc, axis=-1, keepdims=True)
    return zc * lax.rsqrt(var + LN_EPS) * g + b


def _matmul_kernel(x_ref, w_ref, o_ref):
    o_ref[...] = _dot(x_ref[...], w_ref[...])


def _inproj(xb, wp, tm, tn):
    n, d = xb.shape
    p = wp.shape[1]
    return pl.pallas_call(
        _matmul_kernel,
        grid=(n // tm, p // tn),
        in_specs=[pl.BlockSpec((tm, d), lambda i, j: (i, 0)),
                  pl.BlockSpec((d, tn), lambda i, j: (0, j))],
        out_specs=pl.BlockSpec((tm, tn), lambda i, j: (i, j)),
        out_shape=jax.ShapeDtypeStruct((n, p), F32),
        compiler_params=_cparams(("parallel", "arbitrary")),
        name="inproj",
    )(xb, wp)


def _seqmix_kernel(up_ref, uph_ref, uc_ref, uch_ref, pw_ref, ps_ref, cw_ref, cb_ref, lg_ref, lb_ref,
                   a_ref, b_ref, pext, hext, *, ts):
    i = pl.program_id(1)
    has_prev = i > 0

    cur = up_ref[...]
    pext[0:HALO, :] = jnp.where(has_prev, uph_ref[...], 0.0)
    pext[HALO:HALO + ts, :] = cur
    pos = i * ts + lax.broadcasted_iota(I32, (ts, 1), 0)
    for gi, w in enumerate(POOL_WINDOWS):
        cols = slice(gi * POOL_GROUP, (gi + 1) * POOL_GROUP)
        acc = cur[:, cols]
        for d in range(1, w):
            acc = acc + pext[HALO - d:HALO - d + ts, cols]
        cnt = jnp.minimum(pos + 1, w).astype(F32)
        pooled = acc / cnt - cur[:, cols]
        mixed = _dot(pooled.astype(BF16), pw_ref[gi])
        a_ref[:, cols] = (mixed * ps_ref[:, cols]).astype(BF16)

    def glu(u):
        return u[:, :CONV_WIDTH] * jax.nn.sigmoid(u[:, CONV_WIDTH:])

    hext[0:HALO, :] = jnp.where(has_prev, glu(uch_ref[...]), 0.0)
    hext[HALO:HALO + ts, :] = glu(uc_ref[...])
    rc = 64
    for c in range(ts // rc):
        base = HALO + c * rc - (CONV_K - 1)
        acc = jnp.zeros((rc, CONV_WIDTH), F32)
        for j in range(CONV_K):
            acc = acc + hext[base + j:base + j + rc, :] * cw_ref[j:j + 1, :]
        acc = acc + cb_ref[...]
        y = _layernorm(acc, lg_ref[...], lb_ref[...])
        b_ref[c * rc:(c + 1) * rc, :] = (y * jax.nn.sigmoid(y)).astype(BF16)


def _seqmix(proj, bsz, seq, ts, pool_w, pool_scale, conv_w, conv_b, ln_g, ln_b):
    n = bsz * seq
    ns = seq // ts
    hb = ts // HALO

    def cur(width, col):
        return pl.BlockSpec((ts, width), lambda b, i: (b * ns + i, col // width))

    def halo(width, col):
        return pl.BlockSpec((HALO, width), lambda b, i: (jnp.maximum((b * ns + i) * hb - 1, 0), col // width))

    def full(shape):
        return pl.BlockSpec(shape, lambda b, i: (0,) * len(shape))

    return pl.pallas_call(
        functools.partial(_seqmix_kernel, ts=ts),
        grid=(bsz, ns),
        in_specs=[cur(POOL_WIDTH, COL_POOL), halo(POOL_WIDTH, COL_POOL),
                  cur(2 * CONV_WIDTH, COL_CONV), halo(2 * CONV_WIDTH, COL_CONV),
                  full((4, POOL_GROUP, POOL_GROUP)), full((1, POOL_WIDTH)),
                  full((HALO, CONV_WIDTH)), full((1, CONV_WIDTH)), full((1, CONV_WIDTH)), full((1, CONV_WIDTH))],
        out_specs=[pl.BlockSpec((ts, POOL_WIDTH), lambda b, i: (b * ns + i, 0)),
                   pl.BlockSpec((ts, CONV_WIDTH), lambda b, i: (b * ns + i, 0))],
        out_shape=[jax.ShapeDtypeStruct((n, POOL_WIDTH), BF16),
                   jax.ShapeDtypeStruct((n, CONV_WIDTH), BF16)],
        scratch_shapes=[pltpu.VMEM((HALO + ts, POOL_WIDTH), F32),
                        pltpu.VMEM((HALO + ts, CONV_WIDTH), F32)],
        compiler_params=_cparams(("parallel", "arbitrary")),
        name="seqmix",
    )(proj, proj, proj, proj, pool_w, pool_scale, conv_w, conv_b, ln_g, ln_b)


def _rope_tables(positions, rot, period, base):
    lane = np.arange(LANES)
    r = lane % period - base
    active = (r >= 0) & (r < rot)
    f = np.where(active, r % (rot // 2), 0)
    sign = np.where(r < rot // 2, -1.0, 1.0).astype(np.float32)
    inv = jnp.power(ROPE_THETA, -jnp.arange(0, rot, 2, dtype=F32) / rot)
    ang = positions.astype(F32)[..., None] * inv
    cos = jnp.where(active, jnp.cos(ang)[..., f], 1.0)
    sin = jnp.where(active, jnp.sin(ang)[..., f] * sign, 0.0)
    return cos, sin


def _rope128(x, c, s, half, first):
    up = pltpu.roll(x, LANES - half, 1)
    down = pltpu.roll(x, half, 1)
    return x * c + jnp.where(first, up, down) * s


def _rmsnorm(x, g, width):
    ms = jnp.sum(x * x, axis=-1, keepdims=True) * (1.0 / width)
    return x * lax.rsqrt(ms + LN_EPS) * g


def _prep_kernel(cq_ref, ck_ref, cv_ref, mq_ref, mkv_ref, iq_ref, ikw_ref, mkr_ref,
                 cd_ref, sd_ref, ci_ref, si_ref, cm_ref, sm_ref,
                 qn_ref, kvn_ref, wuq_ref, wuk_ref, wuv_ref,
                 qd_ref, kd_ref, vd_ref, qi_ref, ki_ref, wi_ref, qm_ref, km_ref, vm_ref):
    lane = lax.broadcasted_iota(I32, (1, LANES), 1)

    cd, sd = cd_ref[0], sd_ref[0]
    first_d = (lane % DSA_HEAD_DIM) < DSA_ROT // 2
    for c in range(4):
        sl = slice(c * LANES, (c + 1) * LANES)
        qd_ref[0, :, sl] = (_rope128(cq_ref[:, sl], cd, sd, DSA_ROT // 2, first_d)
                            * DSA_HEAD_DIM ** -0.5).astype(BF16)
        kd_ref[0, :, sl] = _rope128(ck_ref[:, sl], cd, sd, DSA_ROT // 2, first_d).astype(BF16)
    vd_ref[0] = cv_ref[...].astype(BF16)

    ci, si = ci_ref[0], si_ref[0]
    first_i = (lane % IDX_DIM) < IDX_ROT // 2
    for c in range(2):
        sl = slice(c * LANES, (c + 1) * LANES)
        qi_ref[0, :, sl] = _rope128(iq_ref[:, sl], ci, si, IDX_ROT // 2, first_i).astype(BF16)
    ikw = ikw_ref[...]
    in_key = lane < IDX_DIM
    kr = _rope128(ikw, jnp.where(in_key, ci, 1.0), jnp.where(in_key, si, 0.0), IDX_ROT // 2, first_i)
    k32 = jnp.where(in_key, kr, 0.0)
    k64 = k32 + pltpu.roll(k32, IDX_DIM, 1)
    k128 = (k64 + pltpu.roll(k64, 2 * IDX_DIM, 1)).astype(BF16)
    ki_ref[0, :, 0:LANES] = k128
    ki_ref[0, :, LANES:2 * LANES] = k128
    wi_ref[0] = pltpu.roll(ikw, LANES - IDX_DIM, 1) * (IDX_HEADS ** -0.5 * IDX_DIM ** -0.5)

    cm, sm = cm_ref[0], sm_ref[0]
    first_m = (lane >= MLA_NOPE) & (lane < MLA_NOPE + MLA_ROPE // 2)
    qn = _rmsnorm(mq_ref[...], qn_ref[...], MLA_Q_RANK).astype(BF16)
    q = _dot(qn, wuq_ref[...])
    for h in range(MLA_HEADS):
        sl = slice(h * LANES, (h + 1) * LANES)
        qm_ref[0, :, sl] = _rope128(q[:, sl], cm, sm, MLA_ROPE // 2, first_m).astype(BF16)
    c_kv = _rmsnorm(mkv_ref[...], kvn_ref[...], MLA_KV_RANK).astype(BF16)
    kn = _dot(c_kv, wuk_ref[...])
    vm_ref[0] = _dot(c_kv, wuv_ref[...]).astype(BF16)
    cmk = pltpu.roll(cm, LANES - MLA_NOPE, 1)
    smk = pltpu.roll(sm, LANES - MLA_NOPE, 1)
    krope = _rope128(mkr_ref[...], cmk, smk, MLA_ROPE // 2, lane < MLA_ROPE // 2)
    krope = pltpu.roll(jnp.where(lane < MLA_ROPE, krope, 0.0), MLA_NOPE, 1)
    for h in range(MLA_HEADS):
        sl = slice(h * LANES, (h + 1) * LANES)
        km_ref[0, :, sl] = (kn[:, sl] + krope).astype(BF16)


def _prep(proj, tables, bsz, seq, ts, qn, kvn, wuq, wuk, wuv):
    ns = seq // ts

    def cur(width, col):
        return pl.BlockSpec((ts, width), lambda b, i: (b * ns + i, col // width))

    def tab():
        return pl.BlockSpec((1, ts, LANES), lambda b, i: (b, i, 0))

    def full(shape):
        return pl.BlockSpec(shape, lambda b, i: (0,) * len(shape))

    def out(width):
        return pl.BlockSpec((1, ts, width), lambda b, i: (b, i, 0))

    widths = (512, 512, 512, 256, 256, LANES, 1024, 1024, 512)
    dtypes = (BF16, BF16, BF16, BF16, BF16, F32, BF16, BF16, BF16)
    return pl.pallas_call(
        _prep_kernel,
        grid=(bsz, ns),
        in_specs=[cur(512, COL_CQ), cur(512, COL_CK), cur(512, COL_CV), cur(512, COL_MQ),
                  cur(256, COL_MKV), cur(256, COL_IQ), cur(LANES, COL_IKW), cur(LANES, COL_MKR)]
                 + [tab() for _ in range(6)]
                 + [full(qn.shape), full(kvn.shape), full(wuq.shape), full(wuk.shape), full(wuv.shape)],
        out_specs=[out(w) for w in widths],
        out_shape=[jax.ShapeDtypeStruct((bsz, seq, w), dt) for w, dt in zip(widths, dtypes)],
        compiler_params=_cparams(("parallel", "arbitrary")),
        name="prep",
    )(*([proj] * 8), *tables, qn, kvn, wuq, wuk, wuv)


def _dsa_kernel(q_ref, k_ref, v_ref, qi_ref, ki_ref, w_ref, o_ref, key_ref, *, tq, tk, n_sel):
    i = pl.program_id(1)
    nkt = ((i + 1) * tq + tk - 1) // tk
    row = i * tq + lax.broadcasted_iota(I32, (tq, 1), 0)
    lane = lax.broadcasted_iota(I32, (1, LANES), 1)
    lane2 = lax.broadcasted_iota(I32, (1, 2 * LANES), 1)

    qi = qi_ref[0]
    w = w_ref[0]
    qi_h = [jnp.where(lane2 // IDX_DIM == h, qi, jnp.zeros_like(qi)) for h in range(IDX_HEADS)]
    w_h = [w[:, h:h + 1] for h in range(IDX_HEADS)]

    def score_tile(kt, carry):
        k0 = pl.multiple_of(kt * tk, tk)
        kit = ki_ref[0, pl.ds(k0, tk), :]
        acc = jnp.zeros((tq, tk), F32)
        for h in range(IDX_HEADS):
            acc = acc + w_h[h] * jnp.maximum(_dot_nt(qi_h[h], kit), 0.0)
        bits = lax.bitcast_convert_type(acc, I32)
        key = jnp.where(bits >= 0, bits, bits ^ 0x7FFFFFFF)
        col = k0 + lax.broadcasted_iota(I32, (1, tk), 1)
        key_ref[:, pl.ds(k0, tk)] = jnp.where(col <= row, key, INT_MIN)
        return carry

    lax.fori_loop(0, nkt, score_tile, 0)

    def count_ge(cand):
        def body(kt, part):
            k0 = pl.multiple_of(kt * tk, tk)
            ind = jnp.where(key_ref[:, pl.ds(k0, tk)] >= cand, 1.0, 0.0)
            for c in range(tk // LANES):
                part = part + ind[:, c * LANES:(c + 1) * LANES]
            return part
        part = lax.fori_loop(0, nkt, body, jnp.zeros((tq, LANES), F32))
        return jnp.sum(part, axis=1, keepdims=True)

    zero = jnp.zeros((tq, 1), I32)
    base = jnp.where(count_ge(zero) >= n_sel, zero, INT_MIN)

    def bit_body(j, base):
        cand = base | jnp.left_shift(jnp.int32(1), 30 - j)
        return jnp.where(count_ge(cand) >= n_sel, cand, base)

    thr = lax.fori_loop(0, 31, bit_body, base)
    thr = jnp.maximum(thr, INT_MIN + 1)

    for p in range(DSA_HEADS // 2):
        cols = slice(p * LANES, (p + 1) * LANES)
        qp = q_ref[0, :, cols]
        o_pair = []
        for hh in range(2):
            qh = jnp.where(lane // DSA_HEAD_DIM == hh, qp, jnp.zeros_like(qp))

            def body(kt, carry, qh=qh, cols=cols):
                m, l, acc = carry
                k0 = pl.multiple_of(kt * tk, tk)
                s = _dot_nt(qh, k_ref[0, pl.ds(k0, tk), cols])
                s = jnp.where(key_ref[:, pl.ds(k0, tk)] >= thr, s, NEG_BIG)
                m_new = jnp.maximum(m, jnp.max(s, axis=1, keepdims=True))
                alpha = jnp.exp(m - m_new)
                pr = jnp.exp(s - m_new)
                l = alpha * l + jnp.sum(pr, axis=1, keepdims=True)
                acc = alpha * acc + _dot(pr.astype(BF16), v_ref[0, pl.ds(k0, tk), cols])
                return m_new, l, acc

            init = (jnp.full((tq, 1), NEG_BIG, F32), jnp.zeros((tq, 1), F32), jnp.zeros((tq, LANES), F32))
            _, l, acc = lax.fori_loop(0, nkt, body, init)
            o_pair.append(acc / l)
        o_ref[0, :, cols] = jnp.where(lane < DSA_HEAD_DIM, o_pair[0], o_pair[1]).astype(BF16)


def _dsa(qd, kd, vd, qi, ki, wi, tq, tk):
    bsz, seq, _ = qd.shape
    n_sel = min(DSA_TOPK, seq // 4)
    one = pl.Buffered(1)

    def blk(width):
        return pl.BlockSpec((1, tq, width), lambda b, i: (b, i, 0))

    def res(width):
        return pl.BlockSpec((1, seq, width), lambda b, i: (b, 0, 0), pipeline_mode=one)

    return pl.pallas_call(
        functools.partial(_dsa_kernel, tq=tq, tk=tk, n_sel=n_sel),
        grid=(bsz, seq // tq),
        in_specs=[blk(512), res(512), res(512), blk(256), res(256), blk(LANES)],
        out_specs=blk(512),
        out_shape=jax.ShapeDtypeStruct((bsz, seq, 512), BF16),
        scratch_shapes=[pltpu.VMEM((tq, seq), I32)],
        compiler_params=_cparams(("parallel", "arbitrary")),
        name="dsa",
    )(qd, kd, vd, qi, ki, wi)


def _mla_kernel(q_ref, k_ref, v_ref, o_ref, *, tq):
    i = pl.program_id(2)
    scale = (MLA_NOPE + MLA_ROPE) ** -0.5
    lane = lax.broadcasted_iota(I32, (1, LANES), 1)
    rowi = lax.broadcasted_iota(I32, (tq, tq), 0)
    coli = lax.broadcasted_iota(I32, (tq, tq), 1)
    o_pair = []
    for hh in range(2):
        cols = slice(hh * LANES, (hh + 1) * LANES)
        qh = q_ref[0, :, cols]

        def step(k0, carry, diag, qh=qh, cols=cols):
            m, l, acc = carry
            s = _dot_nt(qh, k_ref[0, pl.ds(k0, tq), cols]) * scale
            if diag:
                s = jnp.where(coli <= rowi, s, NEG_BIG)
            m_new = jnp.maximum(m, jnp.max(s, axis=1, keepdims=True))
            alpha = jnp.exp(m - m_new)
            pr = jnp.exp(s - m_new)
            l = alpha * l + jnp.sum(pr, axis=1, keepdims=True)
            acc = alpha * acc + _dot(pr.astype(BF16), v_ref[0, pl.ds(k0, tq), :])
            return m_new, l, acc

        init = (jnp.full((tq, 1), NEG_BIG, F32), jnp.zeros((tq, 1), F32), jnp.zeros((tq, LANES), F32))
        carry = lax.fori_loop(0, i, lambda kt, c: step(pl.multiple_of(kt * tq, tq), c, False), init)
        _, l, acc = step(pl.multiple_of(i * tq, tq), carry, True)
        o_pair.append(acc / l)
    o_ref[0] = jnp.where(lane < MLA_V, o_pair[0], o_pair[1]).astype(BF16)


def _mla(qm, km, vm, tq):
    bsz, seq, _ = qm.shape
    return pl.pallas_call(
        functools.partial(_mla_kernel, tq=tq),
        grid=(bsz, MLA_HEADS // 2, seq // tq),
        in_specs=[pl.BlockSpec((1, tq, 2 * LANES), lambda b, p, i: (b, i, p)),
                  pl.BlockSpec((1, seq, 2 * LANES), lambda b, p, i: (b, 0, p)),
                  pl.BlockSpec((1, seq, LANES), lambda b, p, i: (b, 0, p))],
        out_specs=pl.BlockSpec((1, tq, LANES), lambda b, p, i: (b, i, p)),
        out_shape=jax.ShapeDtypeStruct((bsz, seq, MLA_HEADS * MLA_V), BF16),
        compiler_params=_cparams(("parallel", "parallel", "arbitrary")),
        name="mla",
    )(qm, km, vm)


def _merge_kernel(a_ref, b_ref, c_ref, d_ref, g_ref, x_ref, wa_ref, wb_ref, wc_ref, wd_ref, wo_ref,
                  lg_ref, lb_ref, h_ref):
    merged = None
    for j, (br, w) in enumerate(((a_ref, wa_ref), (b_ref, wb_ref), (c_ref, wc_ref), (d_ref, wd_ref))):
        y = jax.nn.sigmoid(g_ref[:, j * D_MODEL:(j + 1) * D_MODEL]) * _dot(br[...], w[...])
        merged = y if merged is None else merged + y
    out = _dot(merged.astype(BF16), wo_ref[...])
    h_ref[...] = _layernorm(DEEPNORM_ALPHA * x_ref[...] + out, lg_ref[...], lb_ref[...])


def _merge(ya, yb, yc, yd, proj, x, wa, wb, wc, wd, wo, lg, lb, tm):
    n = x.shape[0]

    def rows(width):
        return pl.BlockSpec((tm, width), lambda i: (i, 0))

    def full(shape):
        return pl.BlockSpec(shape, lambda i: (0,) * len(shape))

    return pl.pallas_call(
        _merge_kernel,
        grid=(n // tm,),
        in_specs=[rows(512), rows(512), rows(512), rows(512), rows(N_BRANCH * D_MODEL), rows(D_MODEL),
                  full(wa.shape), full(wb.shape), full(wc.shape), full(wd.shape), full(wo.shape),
                  full(lg.shape), full(lb.shape)],
        out_specs=rows(D_MODEL),
        out_shape=jax.ShapeDtypeStruct((n, D_MODEL), F32),
        compiler_params=_cparams(("parallel",)),
        name="merge",
    )(ya, yb, yc, yd, proj, x, wa, wb, wc, wd, wo, lg, lb)


def _router_kernel(h_ref, wh_ref, wl_ref, b_ref, eidx_ref, gate_ref, rank_ref, cnt_ref, carry_ref, *, tm):
    i = pl.program_id(0)

    @pl.when(i == 0)
    def _():
        carry_ref[...] = jnp.zeros_like(carry_ref)

    h = h_ref[...]
    hh = h.astype(BF16)
    hl = (h - hh.astype(F32)).astype(BF16)
    wh = wh_ref[...]
    logits = _dot(hh, wh) + _dot(hl, wh) + _dot(hh, wl_ref[...]) + b_ref[...]
    lane = lax.broadcasted_iota(I32, (1, LANES), 1)
    lane_f = lane.astype(F32)
    vals, idxs = [], []
    cur = logits
    for _ in range(TOP_K):
        m = jnp.max(cur, axis=1, keepdims=True)
        ix = jnp.min(jnp.where(cur == m, lane_f, float(LANES)), axis=1, keepdims=True)
        vals.append(m)
        idxs.append(ix)
        cur = jnp.where(lane_f == ix, -jnp.inf, cur)
    ex = [jnp.exp(v - vals[0]) for v in vals]
    den = ex[0] + ex[1] + ex[2] + ex[3]
    hot = [lane_f == ix for ix in idxs]
    cnt_tok = sum(jnp.where(hk, 1.0, 0.0) for hk in hot)
    r_i = lax.broadcasted_iota(I32, (tm, tm), 0)
    c_i = lax.broadcasted_iota(I32, (tm, tm), 1)
    lower = jnp.where(c_i < r_i, 1.0, 0.0).astype(BF16)
    prefix = _dot(lower, cnt_tok.astype(BF16)) + carry_ref[...]
    eidx = jnp.zeros((tm, LANES), F32)
    gate = jnp.zeros((tm, LANES), F32)
    rank = jnp.zeros((tm, LANES), F32)
    for k in range(TOP_K):
        rk = jnp.sum(jnp.where(hot[k], prefix, 0.0), axis=1, keepdims=True)
        eidx = jnp.where(lane == k, idxs[k], eidx)
        gate = jnp.where(lane == k, ex[k] / den, gate)
        rank = jnp.where(lane == k, rk, rank)
    eidx_ref[...] = eidx.astype(I32)
    gate_ref[...] = gate
    rank_ref[...] = rank.astype(I32)
    carry_ref[...] = carry_ref[...] + jnp.sum(cnt_tok, axis=0, keepdims=True)
    cnt_ref[...] = carry_ref[...].astype(I32)


def _router(h, wh, wl, b, tm):
    n = h.shape[0]

    def rows(width):
        return pl.BlockSpec((tm, width), lambda i: (i, 0))

    def full(shape):
        return pl.BlockSpec(shape, lambda i: (0,) * len(shape))

    return pl.pallas_call(
        functools.partial(_router_kernel, tm=tm),
        grid=(n // tm,),
        in_specs=[rows(D_MODEL), full(wh.shape), full(wl.shape), full(b.shape)],
        out_specs=[rows(LANES), rows(LANES), rows(LANES), full((1, LANES))],
        out_shape=[jax.ShapeDtypeStruct((n, LANES), I32), jax.ShapeDtypeStruct((n, LANES), F32),
                   jax.ShapeDtypeStruct((n, LANES), I32), jax.ShapeDtypeStruct((1, LANES), I32)],
        scratch_shapes=[pltpu.VMEM((1, LANES), F32)],
        compiler_params=_cparams(("arbitrary",)),
        name="router",
    )(h, wh, wl, b)


def _row_copy(src_hbm, row, buf, slot, r, sem):
    return pltpu.make_async_copy(src_hbm.at[pl.ds(row, 1), :], buf.at[slot, pl.ds(r, 1), :], sem.at[slot])


def _gather_rows(idx_vmem, idx_smem, isem, src_hbm, buf, slot, sem, nrows):
    cp = pltpu.make_async_copy(idx_vmem.at[0, 0], idx_smem, isem)
    cp.start()
    cp.wait()

    def body(r, carry):
        _row_copy(src_hbm, idx_smem[r], buf, slot, r, sem).start()
        return carry

    lax.fori_loop(0, nrows, body, 0, unroll=8)


def _wait_rows(src_hbm, buf, slot, sem, nrows):
    def body(r, carry):
        _row_copy(src_hbm, 0, buf, slot, r, sem).wait()
        return carry

    lax.fori_loop(0, nrows, body, 0)


def _expert_kernel(be_ref, idx0_ref, idxn_ref, h_hbm, wgu_ref, bgu_ref, wd_ref, bd_ref, y_ref,
                   xbuf, idx_smem, sem, isem, *, rb, nb):
    del be_ref
    i = pl.program_id(0)
    slot = i % 2

    @pl.when(i == 0)
    def _():
        _gather_rows(idx0_ref, idx_smem, isem, h_hbm, xbuf, 0, sem, rb)

    @pl.when(i + 1 < nb)
    def _():
        _gather_rows(idxn_ref, idx_smem, isem, h_hbm, xbuf, 1 - slot, sem, rb)

    _wait_rows(h_hbm, xbuf, slot, sem, rb)
    xb = xbuf[slot].astype(BF16)
    fc = 512
    y = None
    for c in range(D_FF // fc):
        g = _dot(xb, wgu_ref[0, :, c * fc:(c + 1) * fc]) + bgu_ref[0, :, c * fc:(c + 1) * fc]
        lin = (_dot(xb, wgu_ref[0, :, D_FF + c * fc:D_FF + (c + 1) * fc])
               + bgu_ref[0, :, D_FF + c * fc:D_FF + (c + 1) * fc])
        g = jnp.minimum(g, SWIGLU_LIMIT)
        lin = jnp.clip(lin, -SWIGLU_LIMIT, SWIGLU_LIMIT)
        act = (g * jax.nn.sigmoid(SWIGLU_ALPHA * g) * (lin + 1.0)).astype(BF16)
        part = _dot(act, wd_ref[0, c * fc:(c + 1) * fc, :])
        y = part if y is None else y + part
    y_ref[...] = y + bd_ref[0]


def _experts(blk_expert, row_src, h, wgu, bgu, wd, bd, rb):
    nb = blk_expert.shape[0]
    grid_spec = pltpu.PrefetchScalarGridSpec(
        num_scalar_prefetch=1,
        grid=(nb,),
        in_specs=[pl.BlockSpec((1, 1, rb), lambda i, be: (0, 0, 0)),
                  pl.BlockSpec((1, 1, rb), lambda i, be: (jnp.minimum(i + 1, nb - 1), 0, 0)),
                  pl.BlockSpec(memory_space=pl.ANY),
                  pl.BlockSpec((1, D_MODEL, 2 * D_FF), lambda i, be: (be[i], 0, 0)),
                  pl.BlockSpec((1, 1, 2 * D_FF), lambda i, be: (be[i], 0, 0)),
                  pl.BlockSpec((1, D_FF, D_MODEL), lambda i, be: (be[i], 0, 0)),
                  pl.BlockSpec((1, 1, D_MODEL), lambda i, be: (be[i], 0, 0))],
        out_specs=pl.BlockSpec((rb, D_MODEL), lambda i, be: (i, 0)),
        scratch_shapes=[pltpu.VMEM((2, rb, D_MODEL), F32),
                        pltpu.SMEM((rb,), I32),
                        pltpu.SemaphoreType.DMA((2,)),
                        pltpu.SemaphoreType.DMA(())],
    )
    return pl.pallas_call(
        functools.partial(_expert_kernel, rb=rb, nb=nb),
        grid_spec=grid_spec,
        out_shape=jax.ShapeDtypeStruct((nb * rb, D_MODEL), F32),
        compiler_params=_cparams(("arbitrary",)),
        name="experts",
    )(blk_expert, row_src, row_src, h, wgu, bgu, wd, bd)


def _combine_kernel(idx0_ref, idxn_ref, y_hbm, gate_ref, h_ref, lg_ref, lb_ref, x_ref, xb_ref,
                    ybuf, idx_smem, sem, isem, *, tm, nt):
    i = pl.program_id(0)
    slot = i % 2

    @pl.when(i == 0)
    def _():
        _gather_rows(idx0_ref, idx_smem, isem, y_hbm, ybuf, 0, sem, TOP_K * tm)

    @pl.when(i + 1 < nt)
    def _():
        _gather_rows(idxn_ref, idx_smem, isem, y_hbm, ybuf, 1 - slot, sem, TOP_K * tm)

    _wait_rows(y_hbm, ybuf, slot, sem, TOP_K * tm)
    gate = gate_ref[...]
    moe = None
    for k in range(TOP_K):
        t = ybuf[slot, k * tm:(k + 1) * tm, :] * gate[:, k:k + 1]
        moe = t if moe is None else moe + t
    out = _layernorm(DEEPNORM_ALPHA * h_ref[...] + moe, lg_ref[...], lb_ref[...])
    x_ref[...] = out
    xb_ref[...] = out.astype(BF16)


def _combine(dest_t, ybuf, gate, h, lg, lb, tm):
    n = h.shape[0]
    nt = n // tm

    def rows(width):
        return pl.BlockSpec((tm, width), lambda i: (i, 0))

    def full(shape):
        return pl.BlockSpec(shape, lambda i: (0,) * len(shape))

    return pl.pallas_call(
        functools.partial(_combine_kernel, tm=tm, nt=nt),
        grid=(nt,),
        in_specs=[pl.BlockSpec((1, 1, TOP_K * tm), lambda i: (0, 0, 0)),
                  pl.BlockSpec((1, 1, TOP_K * tm), lambda i: (jnp.minimum(i + 1, nt - 1), 0, 0)),
                  pl.BlockSpec(memory_space=pl.ANY),
                  rows(LANES), rows(D_MODEL), full(lg.shape), full(lb.shape)],
        out_specs=[rows(D_MODEL), rows(D_MODEL)],
        out_shape=[jax.ShapeDtypeStruct((n, D_MODEL), F32), jax.ShapeDtypeStruct((n, D_MODEL), BF16)],
        scratch_shapes=[pltpu.VMEM((2, TOP_K * tm, D_MODEL), F32),
                        pltpu.SMEM((TOP_K * tm,), I32),
                        pltpu.SemaphoreType.DMA((2,)),
                        pltpu.SemaphoreType.DMA(())],
        compiler_params=_cparams(("arbitrary",)),
        name="combine",
    )(dest_t, dest_t, ybuf, gate, h, lg, lb)


def _pad_cols(w, width):
    return jnp.pad(w, ((0, 0), (0, width - w.shape[1])))


def _pack_w_in(w_in):
    offs = np.cumsum((0,) + SPLITS)
    seg = [w_in[:, int(offs[j]):int(offs[j + 1])] for j in range(len(SPLITS))]
    u_pool, u_conv, c_q, c_k, c_v, i_q, i_k, i_w, m_q, m_kv, m_kr, g_all = seg
    parts = [g_all, u_conv, u_pool, c_q, c_k, c_v, _pad_cols(m_q, 512), m_kv, i_q,
             _pad_cols(jnp.concatenate([i_k, i_w], axis=1), LANES), _pad_cols(m_kr, LANES)]
    return jnp.concatenate(parts, axis=1).astype(BF16)


def _pack_heads(w, widths_in, total):
    r = w.shape[0]
    per = sum(widths_in)
    w = w.reshape(r, -1, per)
    w = jnp.pad(w, ((0, 0), (0, 0), (0, total - per)))
    return w.reshape(r, -1)


def _layer(x, xb, tables, bsz, seq, cfg, w_in, pool_w, pool_scale, pool_out, conv_w, conv_b, conv_ln_g,
           conv_ln_b, conv_out, dsa_out, mla_q_norm, mla_kv_norm, mla_wuq, mla_wuk, mla_wuv, mla_out, w_o,
           ln1_g, ln1_b, router_w, router_b, exp_w_gu, exp_b_gu, exp_w_d, exp_b_d, ln2_g, ln2_b):
    n = bsz * seq
    row = lambda v: v.reshape(1, -1).astype(F32)

    proj = _inproj(xb, _pack_w_in(w_in), cfg["tm_in"], cfg["tn_in"])

    ya, yb = _seqmix(proj, bsz, seq, cfg["ts"], pool_w.astype(BF16), row(pool_scale),
                     jnp.pad(conv_w, ((0, HALO - CONV_K), (0, 0))), row(conv_b), row(conv_ln_g), row(conv_ln_b))

    wuq = jnp.pad(_pack_heads(mla_wuq, (MLA_NOPE, MLA_ROPE), LANES), ((0, 512 - MLA_Q_RANK), (0, 0))).astype(BF16)
    wuk = _pack_heads(mla_wuk, (MLA_NOPE,), LANES).astype(BF16)
    qd, kd, vd, qi, ki, wi, qm, km, vm = _prep(
        proj, tables, bsz, seq, cfg["ts"], _pad_cols(row(mla_q_norm), 512), row(mla_kv_norm),
        wuq, wuk, mla_wuv.astype(BF16))

    yc = _dsa(qd, kd, vd, qi, ki, wi, cfg["tq_dsa"], cfg["tk_dsa"]).reshape(n, -1)
    yd = _mla(qm, km, vm, cfg["tq_mla"]).reshape(n, -1)

    h = _merge(ya, yb, yc, yd, proj, x, pool_out.astype(BF16), conv_out.astype(BF16), dsa_out.astype(BF16),
               mla_out.astype(BF16), w_o.astype(BF16), row(ln1_g), row(ln1_b), cfg["tm_merge"])

    rw = _pad_cols(router_w, LANES)
    rwh = rw.astype(BF16)
    rwl = (rw - rwh.astype(F32)).astype(BF16)
    rb_ = jnp.pad(row(router_b), ((0, 0), (0, LANES - N_EXPERTS)), constant_values=NEG_BIG)
    eidx, gate, rank, counts = _router(h, rwh, rwl, rb_, cfg["tm_router"])

    counts = counts[0, :N_EXPERTS]
    padded = (counts + MOE_BLOCK - 1) // MOE_BLOCK * MOE_BLOCK
    pad_end = jnp.cumsum(padded)
    pad_start = pad_end - padded
    n_blocks = -(-(n * TOP_K) // MOE_BLOCK) + N_EXPERTS
    dest = pad_start[eidx[:, :TOP_K]] + rank[:, :TOP_K]
    tok = jnp.broadcast_to(jnp.arange(n, dtype=I32)[:, None], (n, TOP_K))
    row_src = jnp.zeros((n_blocks * MOE_BLOCK,), I32).at[dest.reshape(-1)].set(tok.reshape(-1))
    blk_expert = jnp.minimum(jnp.searchsorted(pad_end, jnp.arange(n_blocks, dtype=I32) * MOE_BLOCK, side='right'),
                             N_EXPERTS - 1).astype(I32)

    ybuf = _experts(blk_expert, row_src.reshape(n_blocks, 1, MOE_BLOCK), h,
                    exp_w_gu.astype(BF16), exp_b_gu.reshape(N_EXPERTS, 1, -1),
                    exp_w_d.astype(BF16), exp_b_d.reshape(N_EXPERTS, 1, -1), MOE_BLOCK)

    tmc = cfg["tm_comb"]
    dest_t = dest.reshape(n // tmc, tmc, TOP_K).transpose(0, 2, 1).reshape(n // tmc, 1, TOP_K * tmc)
    return _combine(dest_t, ybuf, gate, h, row(ln2_g), row(ln2_b), tmc)


_CFG = dict(tm_in=1024, tn_in=768, ts=512, tq_dsa=128, tk_dsa=512, tq_mla=256, tm_merge=256,
            tm_router=512, tm_comb=128)


def _forward(cfg, x, positions, *weights):
    bsz, seq, d = x.shape
    n = bsz * seq
    tables = (*_rope_tables(positions, DSA_ROT, DSA_HEAD_DIM, 0),
              *_rope_tables(positions, IDX_ROT, IDX_DIM, 0),
              *_rope_tables(positions, MLA_ROPE, LANES, MLA_NOPE))
    xf = x.reshape(n, d)
    xb = xf.astype(BF16)
    for l in range(DEPTH):
        xf, xb = _layer(xf, xb, tables, bsz, seq, cfg, *[w[l] for w in weights])
    return xf.reshape(bsz, seq, d)


def kernel(x, positions, w_in, pool_w, pool_scale, pool_out, conv_w, conv_b, conv_ln_g, conv_ln_b, conv_out,
           dsa_out, mla_q_norm, mla_kv_norm, mla_wuq, mla_wuk, mla_wuv, mla_out, w_o, ln1_g, ln1_b, router_w,
           router_b, exp_w_gu, exp_b_gu, exp_w_d, exp_b_d, ln2_g, ln2_b):
    return _forward(_CFG, x, positions, w_in, pool_w, pool_scale, pool_out, conv_w, conv_b, conv_ln_g,
                    conv_ln_b, conv_out, dsa_out, mla_q_norm, mla_kv_norm, mla_wuq, mla_wuk, mla_wuv, mla_out,
                    w_o, ln1_g, ln1_b, router_w, router_b, exp_w_gu, exp_b_gu, exp_w_d, exp_b_d, ln2_g, ln2_b)
```

```python
import functools

import numpy as np
import jax
import jax.numpy as jnp
from jax import lax
from jax.experimental import pallas as pl
from jax.experimental.pallas import tpu as pltpu

F32 = jnp.float32
BF16 = jnp.bfloat16
I32 = jnp.int32

D_MODEL = 1024
DEPTH = 2
ROPE_THETA = 500000.0
LN_EPS = 1e-5
POOL_WIDTH = 512
POOL_WINDOWS = (2, 4, 8, 16)
POOL_GROUP = POOL_WIDTH // 4
CONV_WIDTH = 512
CONV_K = 31
DSA_HEADS = 8
DSA_HEAD_DIM = 64
DSA_ROT = DSA_HEAD_DIM // 4
IDX_HEADS = 8
IDX_DIM = 32
IDX_ROT = IDX_DIM // 4
DSA_TOPK = 256
MLA_HEADS = 8
MLA_NOPE = 64
MLA_ROPE = 32
MLA_V = 64
MLA_Q_RANK = 384
MLA_KV_RANK = 256
N_BRANCH = 4
N_EXPERTS = 32
TOP_K = 4
D_FF = 1024
SWIGLU_ALPHA = 1.702
SWIGLU_LIMIT = 7.0
MOE_BLOCK = 512
DEEPNORM_ALPHA = (2 * DEPTH) ** 0.25
SPLITS = (POOL_WIDTH, 2 * CONV_WIDTH,
          DSA_HEADS * DSA_HEAD_DIM, DSA_HEADS * DSA_HEAD_DIM, DSA_HEADS * DSA_HEAD_DIM,
          IDX_HEADS * IDX_DIM, IDX_DIM, IDX_HEADS,
          MLA_Q_RANK, MLA_KV_RANK, MLA_ROPE,
          N_BRANCH * D_MODEL)

LANES = 128
HALO = 32
VMEM_LIMIT = 56 * 1024 * 1024

COL_GATE = 0
COL_CONV = 4096
COL_POOL = 5120
COL_CQ = 5632
COL_CK = 6144
COL_CV = 6656
COL_MQ = 7168
COL_MKV = 7680
COL_IQ = 7936
COL_IKW = 8192
COL_MKR = 8320
P_COLS = 8448

INT_MIN = -(2 ** 31)
NEG_BIG = -1e30


def _cparams(sem, vmem=VMEM_LIMIT):
    return pltpu.CompilerParams(dimension_semantics=sem, vmem_limit_bytes=vmem)


def _dot(a, b):
    return jnp.dot(a, b, preferred_element_type=F32)


def _dot_nt(a, b):
    return lax.dot_general(a, b, (((1,), (1,)), ((), ())), preferred_element_type=F32)


def _layernorm(z, g, b):
    mu = jnp.mean(z, axis=-1, keepdims=True)
    zc = z - mu
    var = jnp.mean(zc * zc, axis=-1, keepdims=True)
    return zc * lax.rsqrt(var + LN_EPS) * g + b


def _matmul_kernel(x_ref, w_ref, o_ref):
    o_ref[...] = _dot(x_ref[...], w_ref[...])


def _inproj(xb, wp, tm, tn):
    n, d = xb.shape
    p = wp.shape[1]
    return pl.pallas_call(
        _matmul_kernel,
        grid=(n // tm, p // tn),
        in_specs=[pl.BlockSpec((tm, d), lambda i, j: (i, 0)),
                  pl.BlockSpec((d, tn), lambda i, j: (0, j))],
        out_specs=pl.BlockSpec((tm, tn), lambda i, j: (i, j)),
        out_shape=jax.ShapeDtypeStruct((n, p), F32),
        compiler_params=_cparams(("parallel", "arbitrary")),
        name="inproj",
    )(xb, wp)


def _seqmix_kernel(up_ref, uph_ref, uc_ref, uch_ref, pw_ref, ps_ref, cw_ref, cb_ref, lg_ref, lb_ref,
                   a_ref, b_ref, pext, hext, *, ts):
    i = pl.program_id(1)
    has_prev = i > 0

    cur = up_ref[...]
    pext[0:HALO, :] = jnp.where(has_prev, uph_ref[...], 0.0)
    pext[HALO:HALO + ts, :] = cur
    pos = i * ts + lax.broadcasted_iota(I32, (ts, 1), 0)
    for gi, w in enumerate(POOL_WINDOWS):
        cols = slice(gi * POOL_GROUP, (gi + 1) * POOL_GROUP)
        acc = cur[:, cols]
        for d in range(1, w):
            acc = acc + pext[HALO - d:HALO - d + ts, cols]
        cnt = jnp.minimum(pos + 1, w).astype(F32)
        pooled = acc / cnt - cur[:, cols]
        mixed = _dot(pooled.astype(BF16), pw_ref[gi])
        a_ref[:, cols] = (mixed * ps_ref[:, cols]).astype(BF16)

    def glu(u):
        return u[:, :CONV_WIDTH] * jax.nn.sigmoid(u[:, CONV_WIDTH:])

    hext[0:HALO, :] = jnp.where(has_prev, glu(uch_ref[...]), 0.0)
    hext[HALO:HALO + ts, :] = glu(uc_ref[...])
    rc = 64
    for c in range(ts // rc):
        base = HALO + c * rc - (CONV_K - 1)
        acc = jnp.zeros((rc, CONV_WIDTH), F32)
        for j in range(CONV_K):
            acc = acc + hext[base + j:base + j + rc, :] * cw_ref[j:j + 1, :]
        acc = acc + cb_ref[...]
        y = _layernorm(acc, lg_ref[...], lb_ref[...])
        b_ref[c * rc:(c + 1) * rc, :] = (y * jax.nn.sigmoid(y)).astype(BF16)


def _seqmix(proj, bsz, seq, ts, pool_w, pool_scale, conv_w, conv_b, ln_g, ln_b):
    n = bsz * seq
    ns = seq // ts
    hb = ts // HALO

    def cur(width, col):
        return pl.BlockSpec((ts, width), lambda b, i: (b * ns + i, col // width))

    def halo(width, col):
        return pl.BlockSpec((HALO, width), lambda b, i: (jnp.maximum((b * ns + i) * hb - 1, 0), col // width))

    def full(shape):
        return pl.BlockSpec(shape, lambda b, i: (0,) * len(shape))

    return pl.pallas_call(
        functools.partial(_seqmix_kernel, ts=ts),
        grid=(bsz, ns),
        in_specs=[cur(POOL_WIDTH, COL_POOL), halo(POOL_WIDTH, COL_POOL),
                  cur(2 * CONV_WIDTH, COL_CONV), halo(2 * CONV_WIDTH, COL_CONV),
                  full((4, POOL_GROUP, POOL_GROUP)), full((1, POOL_WIDTH)),
                  full((HALO, CONV_WIDTH)), full((1, CONV_WIDTH)), full((1, CONV_WIDTH)), full((1, CONV_WIDTH))],
        out_specs=[pl.BlockSpec((ts, POOL_WIDTH), lambda b, i: (b * ns + i, 0)),
                   pl.BlockSpec((ts, CONV_WIDTH), lambda b, i: (b * ns + i, 0))],
        out_shape=[jax.ShapeDtypeStruct((n, POOL_WIDTH), BF16),
                   jax.ShapeDtypeStruct((n, CONV_WIDTH), BF16)],
        scratch_shapes=[pltpu.VMEM((HALO + ts, POOL_WIDTH), F32),
                        pltpu.VMEM((HALO + ts, CONV_WIDTH), F32)],
        compiler_params=_cparams(("parallel", "arbitrary")),
        name="seqmix",
    )(proj, proj, proj, proj, pool_w, pool_scale, conv_w, conv_b, ln_g, ln_b)


def _rope_tables(positions, rot, period, base):
    lane = np.arange(LANES)
    r = lane % period - base
    active = (r >= 0) & (r < rot)
    f = np.where(active, r % (rot // 2), 0)
    sign = np.where(r < rot // 2, -1.0, 1.0).astype(np.float32)
    inv = jnp.power(ROPE_THETA, -jnp.arange(0, rot, 2, dtype=F32) / rot)
    ang = positions.astype(F32)[..., None] * inv
    cos = jnp.where(active, jnp.cos(ang)[..., f], 1.0)
    sin = jnp.where(active, jnp.sin(ang)[..., f] * sign, 0.0)
    return cos, sin


def _rope128(x, c, s, half, first):
    up = pltpu.roll(x, LANES - half, 1)
    down = pltpu.roll(x, half, 1)
    return x * c + jnp.where(first, up, down) * s


def _rmsnorm(x, g, width):
    ms = jnp.sum(x * x, axis=-1, keepdims=True) * (1.0 / width)
    return x * lax.rsqrt(ms + LN_EPS) * g


def _prep_kernel(cq_ref, ck_ref, cv_ref, mq_ref, mkv_ref, iq_ref, ikw_ref, mkr_ref,
                 cd_ref, sd_ref, ci_ref, si_ref, cm_ref, sm_ref,
                 qn_ref, kvn_ref, wuq_ref, wuk_ref, wuv_ref,
                 qd_ref, kd_ref, vd_ref, qi_ref, ki_ref, wi_ref, qm_ref, km_ref, vm_ref):
    lane = lax.broadcasted_iota(I32, (1, LANES), 1)

    cd, sd = cd_ref[0], sd_ref[0]
    first_d = (lane % DSA_HEAD_DIM) < DSA_ROT // 2
    for c in range(4):
        sl = slice(c * LANES, (c + 1) * LANES)
        qd_ref[0, :, sl] = (_rope128(cq_ref[:, sl], cd, sd, DSA_ROT // 2, first_d)
                            * DSA_HEAD_DIM ** -0.5).astype(BF16)
        kd_ref[0, :, sl] = _rope128(ck_ref[:, sl], cd, sd, DSA_ROT // 2, first_d).astype(BF16)
    vd_ref[0] = cv_ref[...].astype(BF16)

    ci, si = ci_ref[0], si_ref[0]
    first_i = (lane % IDX_DIM) < IDX_ROT // 2
    for c in range(2):
        sl = slice(c * LANES, (c + 1) * LANES)
        qi_ref[0, :, sl] = _rope128(iq_ref[:, sl], ci, si, IDX_ROT // 2, first_i).astype(BF16)
    ikw = ikw_ref[...]
    in_key = lane < IDX_DIM
    kr = _rope128(ikw, jnp.where(in_key, ci, 1.0), jnp.where(in_key, si, 0.0), IDX_ROT // 2, first_i)
    k32 = jnp.where(in_key, kr, 0.0)
    k64 = k32 + pltpu.roll(k32, IDX_DIM, 1)
    k128 = (k64 + pltpu.roll(k64, 2 * IDX_DIM, 1)).astype(BF16)
    ki_ref[0, :, 0:LANES] = k128
    ki_ref[0, :, LANES:2 * LANES] = k128
    wi_ref[0] = pltpu.roll(ikw, LANES - IDX_DIM, 1) * (IDX_HEADS ** -0.5 * IDX_DIM ** -0.5)

    cm, sm = cm_ref[0], sm_ref[0]
    first_m = (lane >= MLA_NOPE) & (lane < MLA_NOPE + MLA_ROPE // 2)
    qn = _rmsnorm(mq_ref[...], qn_ref[...], MLA_Q_RANK).astype(BF16)
    q = _dot(qn, wuq_ref[...])
    for h in range(MLA_HEADS):
        sl = slice(h * LANES, (h + 1) * LANES)
        qm_ref[0, :, sl] = _rope128(q[:, sl], cm, sm, MLA_ROPE // 2, first_m).astype(BF16)
    c_kv = _rmsnorm(mkv_ref[...], kvn_ref[...], MLA_KV_RANK).astype(BF16)
    kn = _dot(c_kv, wuk_ref[...])
    vm_ref[0] = _dot(c_kv, wuv_ref[...]).astype(BF16)
    cmk = pltpu.roll(cm, LANES - MLA_NOPE, 1)
    smk = pltpu.roll(sm, LANES - MLA_NOPE, 1)
    krope = _rope128(mkr_ref[...], cmk, smk, MLA_ROPE // 2, lane < MLA_ROPE // 2)
    krope = pltpu.roll(jnp.where(lane < MLA_ROPE, krope, 0.0), MLA_NOPE, 1)
    for h in range(MLA_HEADS):
        sl = slice(h * LANES, (h + 1) * LANES)
        km_ref[0, :, sl] = (kn[:, sl] + krope).astype(BF16)


def _prep(proj, tables, bsz, seq, ts, qn, kvn, wuq, wuk, wuv):
    ns = seq // ts

    def cur(width, col):
        return pl.BlockSpec((ts, width), lambda b, i: (b * ns + i, col // width))

    def tab():
        return pl.BlockSpec((1, ts, LANES), lambda b, i: (b, i, 0))

    def full(shape):
        return pl.BlockSpec(shape, lambda b, i: (0,) * len(shape))

    def out(width):
        return pl.BlockSpec((1, ts, width), lambda b, i: (b, i, 0))

    widths = (512, 512, 512, 256, 256, LANES, 1024, 1024, 512)
    dtypes = (BF16, BF16, BF16, BF16, BF16, F32, BF16, BF16, BF16)
    return pl.pallas_call(
        _prep_kernel,
        grid=(bsz, ns),
        in_specs=[cur(512, COL_CQ), cur(512, COL_CK), cur(512, COL_CV), cur(512, COL_MQ),
                  cur(256, COL_MKV), cur(256, COL_IQ), cur(LANES, COL_IKW), cur(LANES, COL_MKR)]
                 + [tab() for _ in range(6)]
                 + [full(qn.shape), full(kvn.shape), full(wuq.shape), full(wuk.shape), full(wuv.shape)],
        out_specs=[out(w) for w in widths],
        out_shape=[jax.ShapeDtypeStruct((bsz, seq, w), dt) for w, dt in zip(widths, dtypes)],
        compiler_params=_cparams(("parallel", "arbitrary")),
        name="prep",
    )(*([proj] * 8), *tables, qn, kvn, wuq, wuk, wuv)


def _softmax_step(s, bias, m, l, acc, v, rc):
    rows, width = s.shape
    ps, ms, ls, als = [], [], [], []
    for r0 in range(0, rows, rc):
        cols = []
        for c0 in range(0, width, LANES):
            sc = s[r0:r0 + rc, c0:c0 + LANES]
            if bias is not None:
                rb = r0 % bias.shape[0]
                sc = sc + bias[rb:rb + rc, c0:c0 + LANES]
            cols.append(sc)
        cmax = functools.reduce(jnp.maximum, cols)
        m_old = m[r0:r0 + rc]
        m_new = jnp.maximum(m_old, jnp.max(cmax, axis=1, keepdims=True))
        alpha = jnp.exp(m_old - m_new)
        pcs = [jnp.exp(sc - m_new) for sc in cols]
        psum = functools.reduce(jnp.add, pcs)
        ls.append(alpha * l[r0:r0 + rc] + jnp.sum(psum, axis=1, keepdims=True))
        ms.append(m_new)
        als.append(alpha)
        ps.append(jnp.concatenate(pcs, axis=1).astype(BF16))
    pv = _dot(jnp.concatenate(ps, axis=0), v)
    return jnp.concatenate(ms, axis=0), jnp.concatenate(ls, axis=0), jnp.concatenate(als, axis=0) * acc + pv


def _dsa_kernel(q_ref, k_ref, v_ref, qi_ref, ki_ref, w_ref, o_ref, key_ref, qs_ref, qh_ref, *, tq, tks, n_sel):
    i = pl.program_id(1)
    nks = ((i + 1) * tq + tks - 1) // tks
    nkt = nks
    row = i * tq + lax.broadcasted_iota(I32, (tq, 1), 0)
    lane = lax.broadcasted_iota(I32, (1, LANES), 1)
    lane2 = lax.broadcasted_iota(I32, (1, 2 * LANES), 1)

    qi = qi_ref[0]
    for h in range(IDX_HEADS):
        qs_ref[h * tq:(h + 1) * tq, :] = jnp.where(lane2 // IDX_DIM == h, qi, jnp.zeros_like(qi))
    for h in range(DSA_HEADS):
        qp = q_ref[0, :, (h // 2) * LANES:(h // 2 + 1) * LANES]
        qh_ref[h // 2, (h % 2) * tq:(h % 2 + 1) * tq, :] = jnp.where(lane // DSA_HEAD_DIM == h % 2, qp,
                                                                     jnp.zeros_like(qp))

    w = w_ref[0]
    w_h = [w[:, h:h + 1] for h in range(IDX_HEADS)]
    hg = 4

    def score_tile(kt, carry):
        k0 = pl.multiple_of(kt * tks, tks)
        kit = ki_ref[0, pl.ds(k0, tks), :]
        acc = jnp.zeros((tq, tks), F32)
        for g in range(IDX_HEADS // hg):
            res = _dot_nt(qs_ref[g * hg * tq:(g + 1) * hg * tq, :], kit)
            for j in range(hg):
                acc = acc + w_h[g * hg + j] * jnp.maximum(res[j * tq:(j + 1) * tq, :], 0.0)
        bits = lax.bitcast_convert_type(acc, I32)
        key = jnp.where(bits >= 0, bits, bits ^ 0x7FFFFFFF)
        col = k0 + lax.broadcasted_iota(I32, (1, tks), 1)
        key_ref[:, pl.ds(k0, tks)] = jnp.where(col <= row, key, INT_MIN)
        return carry

    lax.fori_loop(0, nks, score_tile, 0)

    def count_ge(cand):
        cand_b = jnp.broadcast_to(cand, (tq, LANES))

        def body(kt, part):
            k0 = pl.multiple_of(kt * tks, tks)
            rows = []
            for r0 in range(0, tq, 64):
                pr = part[r0:r0 + 64]
                for c in range(tks // LANES):
                    kk = key_ref[r0:r0 + 64, pl.ds(k0 + c * LANES, LANES)]
                    pr = pr + jnp.where(kk >= cand_b[r0:r0 + 64], 1.0, 0.0)
                rows.append(pr)
            return jnp.concatenate(rows, axis=0)
        part = lax.fori_loop(0, nkt, body, jnp.zeros((tq, LANES), F32))
        return jnp.sum(part, axis=1, keepdims=True)

    zero = jnp.zeros((tq, 1), I32)
    base = jnp.where(count_ge(zero) >= n_sel, zero, INT_MIN)

    def bit_body(j, base):
        cand = base | jnp.left_shift(jnp.int32(1), 30 - j)
        return jnp.where(count_ge(cand) >= n_sel, cand, base)

    thr = lax.fori_loop(0, 31, bit_body, base)
    thr = jnp.maximum(thr, INT_MIN + 1)

    def att_tile(kt, carry):
        k0 = pl.multiple_of(kt * tks, tks)
        bias = jnp.where(key_ref[:, pl.ds(k0, tks)] >= thr, 0.0, NEG_BIG)
        out = []
        for p in range(DSA_HEADS // 2):
            cols = slice(p * LANES, (p + 1) * LANES)
            m, l, acc = carry[p]
            s = _dot_nt(qh_ref[p], k_ref[0, pl.ds(k0, tks), cols])
            out.append(_softmax_step(s, bias, m, l, acc, v_ref[0, pl.ds(k0, tks), cols], 64))
        return tuple(out)

    init = tuple((jnp.full((2 * tq, LANES), NEG_BIG, F32), jnp.zeros((2 * tq, LANES), F32),
                  jnp.zeros((2 * tq, LANES), F32)) for _ in range(DSA_HEADS // 2))
    fin = lax.fori_loop(0, nkt, att_tile, init)
    for p in range(DSA_HEADS // 2):
        o = fin[p][2] / fin[p][1]
        o_ref[0, :, p * LANES:(p + 1) * LANES] = jnp.where(lane < DSA_HEAD_DIM, o[:tq], o[tq:]).astype(BF16)


def _dsa(qd, kd, vd, qi, ki, wi, tq, tks):
    bsz, seq, _ = qd.shape
    n_sel = min(DSA_TOPK, seq // 4)
    one = pl.Buffered(1)

    def blk(width):
        return pl.BlockSpec((1, tq, width), lambda b, i: (b, i, 0))

    def res(width):
        return pl.BlockSpec((1, seq, width), lambda b, i: (b, 0, 0), pipeline_mode=one)

    return pl.pallas_call(
        functools.partial(_dsa_kernel, tq=tq, tks=tks, n_sel=n_sel),
        grid=(bsz, seq // tq),
        in_specs=[blk(512), res(512), res(512), blk(256), res(256), blk(LANES)],
        out_specs=blk(512),
        out_shape=jax.ShapeDtypeStruct((bsz, seq, 512), BF16),
        scratch_shapes=[pltpu.VMEM((tq, seq), I32),
                        pltpu.VMEM((IDX_HEADS * tq, 2 * LANES), BF16),
                        pltpu.VMEM((DSA_HEADS // 2, 2 * tq, LANES), BF16)],
        compiler_params=_cparams(("parallel", "arbitrary")),
        name="dsa",
    )(qd, kd, vd, qi, ki, wi)


def _mla_kernel(q_ref, k_ref, v_ref, o_ref, *, tq):
    i = pl.program_id(2)
    scale = (MLA_NOPE + MLA_ROPE) ** -0.5
    lane = lax.broadcasted_iota(I32, (1, LANES), 1)

    def step(k0, carry, diag):
        vt = v_ref[0, pl.ds(k0, tq), :]
        out = []
        for hh in range(2):
            cols = slice(hh * LANES, (hh + 1) * LANES)
            m, l, acc = carry[hh]
            s = _dot_nt(q_ref[0, :, cols], k_ref[0, pl.ds(k0, tq), cols]) * scale
            bias = None
            if diag:
                rowi = lax.broadcasted_iota(I32, (tq, tq), 0)
                coli = lax.broadcasted_iota(I32, (tq, tq), 1)
                bias = jnp.where(coli <= rowi, 0.0, NEG_BIG)
            out.append(_softmax_step(s, bias, m, l, acc, vt, 64))
        return tuple(out)

    init = tuple((jnp.full((tq, LANES), NEG_BIG, F32), jnp.zeros((tq, LANES), F32), jnp.zeros((tq, LANES), F32))
                 for _ in range(2))
    carry = lax.fori_loop(0, i, lambda kt, c: step(pl.multiple_of(kt * tq, tq), c, False), init)
    fin = step(pl.multiple_of(i * tq, tq), carry, True)
    o_ref[0] = jnp.where(lane < MLA_V, fin[0][2] / fin[0][1], fin[1][2] / fin[1][1]).astype(BF16)


def _mla(qm, km, vm, tq):
    bsz, seq, _ = qm.shape
    return pl.pallas_call(
        functools.partial(_mla_kernel, tq=tq),
        grid=(bsz, MLA_HEADS // 2, seq // tq),
        in_specs=[pl.BlockSpec((1, tq, 2 * LANES), lambda b, p, i: (b, i, p)),
                  pl.BlockSpec((1, seq, 2 * LANES), lambda b, p, i: (b, 0, p)),
                  pl.BlockSpec((1, seq, LANES), lambda b, p, i: (b, 0, p))],
        out_specs=pl.BlockSpec((1, tq, LANES), lambda b, p, i: (b, i, p)),
        out_shape=jax.ShapeDtypeStruct((bsz, seq, MLA_HEADS * MLA_V), BF16),
        compiler_params=_cparams(("parallel", "parallel", "arbitrary")),
        name="mla",
    )(qm, km, vm)


def _merge_kernel(a_ref, b_ref, c_ref, d_ref, g_ref, x_ref, wa_ref, wb_ref, wc_ref, wd_ref, wo_ref,
                  lg_ref, lb_ref, h_ref):
    merged = None
    for j, (br, w) in enumerate(((a_ref, wa_ref), (b_ref, wb_ref), (c_ref, wc_ref), (d_ref, wd_ref))):
        y = jax.nn.sigmoid(g_ref[:, j * D_MODEL:(j + 1) * D_MODEL]) * _dot(br[...], w[...])
        merged = y if merged is None else merged + y
    out = _dot(merged.astype(BF16), wo_ref[...])
    h_ref[...] = _layernorm(DEEPNORM_ALPHA * x_ref[...] + out, lg_ref[...], lb_ref[...])


def _merge(ya, yb, yc, yd, proj, x, wa, wb, wc, wd, wo, lg, lb, tm):
    n = x.shape[0]

    def rows(width):
        return pl.BlockSpec((tm, width), lambda i: (i, 0))

    def full(shape):
        return pl.BlockSpec(shape, lambda i: (0,) * len(shape))

    return pl.pallas_call(
        _merge_kernel,
        grid=(n // tm,),
        in_specs=[rows(512), rows(512), rows(512), rows(512), rows(N_BRANCH * D_MODEL), rows(D_MODEL),
                  full(wa.shape), full(wb.shape), full(wc.shape), full(wd.shape), full(wo.shape),
                  full(lg.shape), full(lb.shape)],
        out_specs=rows(D_MODEL),
        out_shape=jax.ShapeDtypeStruct((n, D_MODEL), F32),
        compiler_params=_cparams(("parallel",)),
        name="merge",
    )(ya, yb, yc, yd, proj, x, wa, wb, wc, wd, wo, lg, lb)


def _router_kernel(h_ref, wh_ref, wl_ref, b_ref, eidx_ref, gate_ref, rank_ref, cnt_ref, carry_ref, *, tm):
    i = pl.program_id(0)

    @pl.when(i == 0)
    def _():
        carry_ref[...] = jnp.zeros_like(carry_ref)

    h = h_ref[...]
    hh = h.astype(BF16)
    hl = (h - hh.astype(F32)).astype(BF16)
    wh = wh_ref[...]
    logits = _dot(hh, wh) + _dot(hl, wh) + _dot(hh, wl_ref[...]) + b_ref[...]
    lane = lax.broadcasted_iota(I32, (1, LANES), 1)
    lane_f = lane.astype(F32)
    vals, idxs = [], []
    cur = logits
    for _ in range(TOP_K):
        m = jnp.max(cur, axis=1, keepdims=True)
        ix = jnp.min(jnp.where(cur == m, lane_f, float(LANES)), axis=1, keepdims=True)
        vals.append(m)
        idxs.append(ix)
        cur = jnp.where(lane_f == ix, -jnp.inf, cur)
    ex = [jnp.exp(v - vals[0]) for v in vals]
    den = ex[0] + ex[1] + ex[2] + ex[3]
    hot = [lane_f == ix for ix in idxs]
    cnt_tok = sum(jnp.where(hk, 1.0, 0.0) for hk in hot)
    r_i = lax.broadcasted_iota(I32, (tm, tm), 0)
    c_i = lax.broadcasted_iota(I32, (tm, tm), 1)
    lower = jnp.where(c_i < r_i, 1.0, 0.0).astype(BF16)
    prefix = _dot(lower, cnt_tok.astype(BF16)) + carry_ref[...]
    eidx = jnp.zeros((tm, LANES), F32)
    gate = jnp.zeros((tm, LANES), F32)
    rank = jnp.zeros((tm, LANES), F32)
    for k in range(TOP_K):
        rk = jnp.sum(jnp.where(hot[k], prefix, 0.0), axis=1, keepdims=True)
        eidx = jnp.where(lane == k, idxs[k], eidx)
        gate = jnp.where(lane == k, ex[k] / den, gate)
        rank = jnp.where(lane == k, rk, rank)
    eidx_ref[...] = eidx.astype(I32)
    gate_ref[...] = gate
    rank_ref[...] = rank.astype(I32)
    carry_ref[...] = carry_ref[...] + jnp.sum(cnt_tok, axis=0, keepdims=True)
    cnt_ref[...] = carry_ref[...].astype(I32)


def _router(h, wh, wl, b, tm):
    n = h.shape[0]

    def rows(width):
        return pl.BlockSpec((tm, width), lambda i: (i, 0))

    def full(shape):
        return pl.BlockSpec(shape, lambda i: (0,) * len(shape))

    return pl.pallas_call(
        functools.partial(_router_kernel, tm=tm),
        grid=(n // tm,),
        in_specs=[rows(D_MODEL), full(wh.shape), full(wl.shape), full(b.shape)],
        out_specs=[rows(LANES), rows(LANES), rows(LANES), full((1, LANES))],
        out_shape=[jax.ShapeDtypeStruct((n, LANES), I32), jax.ShapeDtypeStruct((n, LANES), F32),
                   jax.ShapeDtypeStruct((n, LANES), I32), jax.ShapeDtypeStruct((1, LANES), I32)],
        scratch_shapes=[pltpu.VMEM((1, LANES), F32)],
        compiler_params=_cparams(("arbitrary",)),
        name="router",
    )(h, wh, wl, b)


def _row_copy(src_hbm, row, buf, slot, r, sem):
    return pltpu.make_async_copy(src_hbm.at[pl.ds(row, 1), :], buf.at[slot, pl.ds(r, 1), :], sem.at[slot])


def _gather_rows(idx_vmem, idx_smem, isem, src_hbm, buf, slot, sem, nrows, inline):
    cp = pltpu.make_async_copy(idx_vmem.at[0, 0], idx_smem, isem)
    cp.start()
    cp.wait()
    if inline:
        for r in range(nrows):
            _row_copy(src_hbm, idx_smem[r], buf, slot, r, sem).start()
        return

    def body(r, carry):
        _row_copy(src_hbm, idx_smem[r], buf, slot, r, sem).start()
        return carry

    lax.fori_loop(0, nrows, body, 0, unroll=8)


def _wait_rows(src_hbm, buf, slot, sem, nrows):
    pltpu.make_async_copy(src_hbm.at[pl.ds(0, nrows), :], buf.at[slot], sem.at[slot]).wait()


def _expert_kernel(be_ref, idx0_ref, idxn_ref, h_hbm, wgu_ref, bgu_ref, wd_ref, bd_ref, y_ref,
                   xbuf, idx_smem, sem, isem, *, rb, nb):
    del be_ref
    i = pl.program_id(0)
    slot = i % 2

    @pl.when(i == 0)
    def _():
        _gather_rows(idx0_ref, idx_smem, isem, h_hbm, xbuf, 0, sem, rb, False)

    _wait_rows(h_hbm, xbuf, slot, sem, rb)
    _gather_rows(idxn_ref, idx_smem, isem, h_hbm, xbuf, 1 - slot, sem, rb, True)
    xb = xbuf[slot].astype(BF16)
    fc = 512
    y = None
    for c in range(D_FF // fc):
        g = _dot(xb, wgu_ref[0, :, c * fc:(c + 1) * fc]) + bgu_ref[0, :, c * fc:(c + 1) * fc]
        lin = (_dot(xb, wgu_ref[0, :, D_FF + c * fc:D_FF + (c + 1) * fc])
               + bgu_ref[0, :, D_FF + c * fc:D_FF + (c + 1) * fc])
        g = jnp.minimum(g, SWIGLU_LIMIT)
        lin = jnp.clip(lin, -SWIGLU_LIMIT, SWIGLU_LIMIT)
        act = (g * jax.nn.sigmoid(SWIGLU_ALPHA * g) * (lin + 1.0)).astype(BF16)
        part = _dot(act, wd_ref[0, c * fc:(c + 1) * fc, :])
        y = part if y is None else y + part
    y_ref[...] = y + bd_ref[0]

    @pl.when(i == nb - 1)
    def _():
        _wait_rows(h_hbm, xbuf, 1 - slot, sem, rb)


def _experts(blk_expert, row_src, h, wgu, bgu, wd, bd, rb):
    nb = blk_expert.shape[0]
    grid_spec = pltpu.PrefetchScalarGridSpec(
        num_scalar_prefetch=1,
        grid=(nb,),
        in_specs=[pl.BlockSpec((1, 1, rb), lambda i, be: (0, 0, 0)),
                  pl.BlockSpec((1, 1, rb), lambda i, be: (jnp.minimum(i + 1, nb - 1), 0, 0)),
                  pl.BlockSpec(memory_space=pl.ANY),
                  pl.BlockSpec((1, D_MODEL, 2 * D_FF), lambda i, be: (be[i], 0, 0)),
                  pl.BlockSpec((1, 1, 2 * D_FF), lambda i, be: (be[i], 0, 0)),
                  pl.BlockSpec((1, D_FF, D_MODEL), lambda i, be: (be[i], 0, 0)),
                  pl.BlockSpec((1, 1, D_MODEL), lambda i, be: (be[i], 0, 0))],
        out_specs=pl.BlockSpec((rb, D_MODEL), lambda i, be: (i, 0)),
        scratch_shapes=[pltpu.VMEM((2, rb, D_MODEL), F32),
                        pltpu.SMEM((rb,), I32),
                        pltpu.SemaphoreType.DMA((2,)),
                        pltpu.SemaphoreType.DMA(())],
    )
    return pl.pallas_call(
        functools.partial(_expert_kernel, rb=rb, nb=nb),
        grid_spec=grid_spec,
        out_shape=jax.ShapeDtypeStruct((nb * rb, D_MODEL), F32),
        compiler_params=_cparams(("arbitrary",)),
        name="experts",
    )(blk_expert, row_src, row_src, h, wgu, bgu, wd, bd)


def _combine_kernel(idx0_ref, idxn_ref, y_hbm, gate_ref, h_ref, lg_ref, lb_ref, x_ref, xb_ref,
                    ybuf, idx_smem, sem, isem, *, tm, nt):
    i = pl.program_id(0)
    slot = i % 2

    @pl.when(i == 0)
    def _():
        _gather_rows(idx0_ref, idx_smem, isem, y_hbm, ybuf, 0, sem, TOP_K * tm, False)

    _wait_rows(y_hbm, ybuf, slot, sem, TOP_K * tm)
    _gather_rows(idxn_ref, idx_smem, isem, y_hbm, ybuf, 1 - slot, sem, TOP_K * tm, True)
    gate = gate_ref[...]
    moe = None
    for k in range(TOP_K):
        t = ybuf[slot, k * tm:(k + 1) * tm, :] * gate[:, k:k + 1]
        moe = t if moe is None else moe + t
    out = _layernorm(DEEPNORM_ALPHA * h_ref[...] + moe, lg_ref[...], lb_ref[...])
    x_ref[...] = out
    xb_ref[...] = out.astype(BF16)

    @pl.when(i == nt - 1)
    def _():
        _wait_rows(y_hbm, ybuf, 1 - slot, sem, TOP_K * tm)


def _combine(dest_t, ybuf, gate, h, lg, lb, tm):
    n = h.shape[0]
    nt = n // tm

    def rows(width):
        return pl.BlockSpec((tm, width), lambda i: (i, 0))

    def full(shape):
        return pl.BlockSpec(shape, lambda i: (0,) * len(shape))

    return pl.pallas_call(
        functools.partial(_combine_kernel, tm=tm, nt=nt),
        grid=(nt,),
        in_specs=[pl.BlockSpec((1, 1, TOP_K * tm), lambda i: (0, 0, 0)),
                  pl.BlockSpec((1, 1, TOP_K * tm), lambda i: (jnp.minimum(i + 1, nt - 1), 0, 0)),
                  pl.BlockSpec(memory_space=pl.ANY),
                  rows(LANES), rows(D_MODEL), full(lg.shape), full(lb.shape)],
        out_specs=[rows(D_MODEL), rows(D_MODEL)],
        out_shape=[jax.ShapeDtypeStruct((n, D_MODEL), F32), jax.ShapeDtypeStruct((n, D_MODEL), BF16)],
        scratch_shapes=[pltpu.VMEM((2, TOP_K * tm, D_MODEL), F32),
                        pltpu.SMEM((TOP_K * tm,), I32),
                        pltpu.SemaphoreType.DMA((2,)),
                        pltpu.SemaphoreType.DMA(())],
        compiler_params=_cparams(("arbitrary",)),
        name="combine",
    )(dest_t, dest_t, ybuf, gate, h, lg, lb)


def _pad_cols(w, width):
    return jnp.pad(w, ((0, 0), (0, width - w.shape[1])))


def _pack_w_in(w_in):
    offs = np.cumsum((0,) + SPLITS)
    seg = [w_in[:, int(offs[j]):int(offs[j + 1])] for j in range(len(SPLITS))]
    u_pool, u_conv, c_q, c_k, c_v, i_q, i_k, i_w, m_q, m_kv, m_kr, g_all = seg
    parts = [g_all, u_conv, u_pool, c_q, c_k, c_v, _pad_cols(m_q, 512), m_kv, i_q,
             _pad_cols(jnp.concatenate([i_k, i_w], axis=1), LANES), _pad_cols(m_kr, LANES)]
    return jnp.concatenate(parts, axis=1).astype(BF16)


def _pack_heads(w, widths_in, total):
    r = w.shape[0]
    per = sum(widths_in)
    w = w.reshape(r, -1, per)
    w = jnp.pad(w, ((0, 0), (0, 0), (0, total - per)))
    return w.reshape(r, -1)


def _layer(x, xb, tables, bsz, seq, cfg, w_in, pool_w, pool_scale, pool_out, conv_w, conv_b, conv_ln_g,
           conv_ln_b, conv_out, dsa_out, mla_q_norm, mla_kv_norm, mla_wuq, mla_wuk, mla_wuv, mla_out, w_o,
           ln1_g, ln1_b, router_w, router_b, exp_w_gu, exp_b_gu, exp_w_d, exp_b_d, ln2_g, ln2_b):
    n = bsz * seq
    row = lambda v: v.reshape(1, -1).astype(F32)

    proj = _inproj(xb, _pack_w_in(w_in), cfg["tm_in"], cfg["tn_in"])

    ya, yb = _seqmix(proj, bsz, seq, cfg["ts"], pool_w.astype(BF16), row(pool_scale),
                     jnp.pad(conv_w, ((0, HALO - CONV_K), (0, 0))), row(conv_b), row(conv_ln_g), row(conv_ln_b))

    wuq = jnp.pad(_pack_heads(mla_wuq, (MLA_NOPE, MLA_ROPE), LANES), ((0, 512 - MLA_Q_RANK), (0, 0))).astype(BF16)
    wuk = _pack_heads(mla_wuk, (MLA_NOPE,), LANES).astype(BF16)
    qd, kd, vd, qi, ki, wi, qm, km, vm = _prep(
        proj, tables, bsz, seq, cfg["ts"], _pad_cols(row(mla_q_norm), 512), row(mla_kv_norm),
        wuq, wuk, mla_wuv.astype(BF16))

    yc = _dsa(qd, kd, vd, qi, ki, wi, cfg["tq_dsa"], cfg["tk_dsa"]).reshape(n, -1)
    yd = _mla(qm, km, vm, cfg["tq_mla"]).reshape(n, -1)

    h = _merge(ya, yb, yc, yd, proj, x, pool_out.astype(BF16), conv_out.astype(BF16), dsa_out.astype(BF16),
               mla_out.astype(BF16), w_o.astype(BF16), row(ln1_g), row(ln1_b), cfg["tm_merge"])

    rw = _pad_cols(router_w, LANES)
    rwh = rw.astype(BF16)
    rwl = (rw - rwh.astype(F32)).astype(BF16)
    rb_ = jnp.pad(row(router_b), ((0, 0), (0, LANES - N_EXPERTS)), constant_values=NEG_BIG)
    eidx, gate, rank, counts = _router(h, rwh, rwl, rb_, cfg["tm_router"])

    counts = counts[0, :N_EXPERTS]
    padded = (counts + MOE_BLOCK - 1) // MOE_BLOCK * MOE_BLOCK
    pad_end = jnp.cumsum(padded)
    pad_start = pad_end - padded
    n_blocks = -(-(n * TOP_K) // MOE_BLOCK) + N_EXPERTS
    dest = pad_start[eidx[:, :TOP_K]] + rank[:, :TOP_K]
    tok = jnp.broadcast_to(jnp.arange(n, dtype=I32)[:, None], (n, TOP_K))
    row_src = jnp.zeros((n_blocks * MOE_BLOCK,), I32).at[dest.reshape(-1)].set(tok.reshape(-1), unique_indices=True)
    blk_expert = jnp.minimum(jnp.searchsorted(pad_end, jnp.arange(n_blocks, dtype=I32) * MOE_BLOCK, side='right'),
                             N_EXPERTS - 1).astype(I32)

    ybuf = _experts(blk_expert, row_src.reshape(n_blocks, 1, MOE_BLOCK), h,
                    exp_w_gu.astype(BF16), exp_b_gu.reshape(N_EXPERTS, 1, -1),
                    exp_w_d.astype(BF16), exp_b_d.reshape(N_EXPERTS, 1, -1), MOE_BLOCK)

    tmc = cfg["tm_comb"]
    dest_t = dest.reshape(n // tmc, tmc, TOP_K).transpose(0, 2, 1).reshape(n // tmc, 1, TOP_K * tmc)
    return _combine(dest_t, ybuf, gate, h, row(ln2_g), row(ln2_b), tmc)


_CFG = dict(tm_in=1024, tn_in=768, ts=512, tq_dsa=256, tk_dsa=512, tq_mla=512, tm_merge=256,
            tm_router=512, tm_comb=128)


def _forward(cfg, x, positions, *weights):
    bsz, seq, d = x.shape
    n = bsz * seq
    tables = (*_rope_tables(positions, DSA_ROT, DSA_HEAD_DIM, 0),
              *_rope_tables(positions, IDX_ROT, IDX_DIM, 0),
              *_rope_tables(positions, MLA_ROPE, LANES, MLA_NOPE))
    xf = x.reshape(n, d)
    xb = xf.astype(BF16)
    for l in range(DEPTH):
        xf, xb = _layer(xf, xb, tables, bsz, seq, cfg, *[w[l] for w in weights])
    return xf.reshape(bsz, seq, d)


def kernel(x, positions, w_in, pool_w, pool_scale, pool_out, conv_w, conv_b, conv_ln_g, conv_ln_b, conv_out,
           dsa_out, mla_q_norm, mla_kv_norm, mla_wuq, mla_wuk, mla_wuv, mla_out, w_o, ln1_g, ln1_b, router_w,
           router_b, exp_w_gu, exp_b_gu, exp_w_d, exp_b_d, ln2_g, ln2_b):
    return _forward(_CFG, x, positions, w_in, pool_w, pool_scale, pool_out, conv_w, conv_b, conv_ln_g,
                    conv_ln_b, conv_out, dsa_out, mla_q_norm, mla_kv_norm, mla_wuq, mla_wuk, mla_wuv, mla_out,
                    w_o, ln1_g, ln1_b, router_w, router_b, exp_w_gu, exp_b_gu, exp_w_d, exp_b_d, ln2_g, ln2_b)
```

```python
import functools

import numpy as np
import jax
import jax.numpy as jnp
from jax import lax
from jax.experimental import pallas as pl
from jax.experimental.pallas import tpu as pltpu

F32 = jnp.float32
BF16 = jnp.bfloat16
I32 = jnp.int32

D_MODEL = 1024
DEPTH = 2
ROPE_THETA = 500000.0
LN_EPS = 1e-5
POOL_WIDTH = 512
POOL_WINDOWS = (2, 4, 8, 16)
POOL_GROUP = POOL_WIDTH // 4
CONV_WIDTH = 512
CONV_K = 31
DSA_HEADS = 8
DSA_HEAD_DIM = 64
DSA_ROT = DSA_HEAD_DIM // 4
IDX_HEADS = 8
IDX_DIM = 32
IDX_ROT = IDX_DIM // 4
DSA_TOPK = 256
MLA_HEADS = 8
MLA_NOPE = 64
MLA_ROPE = 32
MLA_V = 64
MLA_Q_RANK = 384
MLA_KV_RANK = 256
N_BRANCH = 4
N_EXPERTS = 32
TOP_K = 4
D_FF = 1024
SWIGLU_ALPHA = 1.702
SWIGLU_LIMIT = 7.0
MOE_BLOCK = 512
DEEPNORM_ALPHA = (2 * DEPTH) ** 0.25
SPLITS = (POOL_WIDTH, 2 * CONV_WIDTH,
          DSA_HEADS * DSA_HEAD_DIM, DSA_HEADS * DSA_HEAD_DIM, DSA_HEADS * DSA_HEAD_DIM,
          IDX_HEADS * IDX_DIM, IDX_DIM, IDX_HEADS,
          MLA_Q_RANK, MLA_KV_RANK, MLA_ROPE,
          N_BRANCH * D_MODEL)

LANES = 128
HALO = 32
VMEM_LIMIT = 56 * 1024 * 1024

COL_GATE = 0
COL_CONV = 4096
COL_POOL = 5120
COL_CQ = 5632
COL_CK = 6144
COL_CV = 6656
COL_MQ = 7168
COL_MKV = 7680
COL_IQ = 7936
COL_IKW = 8192
COL_MKR = 8320
P_COLS = 8448

INT_MIN = -(2 ** 31)
NEG_BIG = -1e30


def _cparams(sem, vmem=VMEM_LIMIT):
    return pltpu.CompilerParams(dimension_semantics=sem, vmem_limit_bytes=vmem)


def _dot(a, b):
    return jnp.dot(a, b, preferred_element_type=F32)


def _dot_nt(a, b):
    return lax.dot_general(a, b, (((1,), (1,)), ((), ())), preferred_element_type=F32)


def _layernorm(z, g, b):
    mu = jnp.mean(z, axis=-1, keepdims=True)
    zc = z - mu
    var = jnp.mean(zc * zc, axis=-1, keepdims=True)
    return zc * lax.rsqrt(var + LN_EPS) * g + b


def _matmul_kernel(x_ref, w_ref, o_ref):
    o_ref[...] = _dot(x_ref[...], w_ref[...])


def _inproj(xb, wp, tm, tn):
    n, d = xb.shape
    p = wp.shape[1]
    return pl.pallas_call(
        _matmul_kernel,
        grid=(n // tm, p // tn),
        in_specs=[pl.BlockSpec((tm, d), lambda i, j: (i, 0)),
                  pl.BlockSpec((d, tn), lambda i, j: (0, j))],
        out_specs=pl.BlockSpec((tm, tn), lambda i, j: (i, j)),
        out_shape=jax.ShapeDtypeStruct((n, p), F32),
        compiler_params=_cparams(("parallel", "arbitrary")),
        name="inproj",
    )(xb, wp)


def _seqmix_kernel(up_ref, uph_ref, uc_ref, uch_ref, pw_ref, ps_ref, cw_ref, cb_ref, lg_ref, lb_ref,
                   a_ref, b_ref, pext, hext, *, ts):
    i = pl.program_id(1)
    has_prev = i > 0

    cur = up_ref[...]
    pext[0:HALO, :] = jnp.where(has_prev, uph_ref[...], 0.0)
    pext[HALO:HALO + ts, :] = cur
    pos = i * ts + lax.broadcasted_iota(I32, (ts, 1), 0)
    for gi, w in enumerate(POOL_WINDOWS):
        cols = slice(gi * POOL_GROUP, (gi + 1) * POOL_GROUP)
        acc = cur[:, cols]
        for d in range(1, w):
            acc = acc + pext[HALO - d:HALO - d + ts, cols]
        cnt = jnp.minimum(pos + 1, w).astype(F32)
        pooled = acc / cnt - cur[:, cols]
        mixed = _dot(pooled.astype(BF16), pw_ref[gi])
        a_ref[:, cols] = (mixed * ps_ref[:, cols]).astype(BF16)

    def glu(u):
        return u[:, :CONV_WIDTH] * jax.nn.sigmoid(u[:, CONV_WIDTH:])

    hext[0:HALO, :] = jnp.where(has_prev, glu(uch_ref[...]), 0.0)
    hext[HALO:HALO + ts, :] = glu(uc_ref[...])
    rc = 64
    for c in range(ts // rc):
        base = HALO + c * rc - (CONV_K - 1)
        acc = jnp.zeros((rc, CONV_WIDTH), F32)
        for j in range(CONV_K):
            acc = acc + hext[base + j:base + j + rc, :] * cw_ref[j:j + 1, :]
        acc = acc + cb_ref[...]
        y = _layernorm(acc, lg_ref[...], lb_ref[...])
        b_ref[c * rc:(c + 1) * rc, :] = (y * jax.nn.sigmoid(y)).astype(BF16)


def _seqmix(proj, bsz, seq, ts, pool_w, pool_scale, conv_w, conv_b, ln_g, ln_b):
    n = bsz * seq
    ns = seq // ts
    hb = ts // HALO

    def cur(width, col):
        return pl.BlockSpec((ts, width), lambda b, i: (b * ns + i, col // width))

    def halo(width, col):
        return pl.BlockSpec((HALO, width), lambda b, i: (jnp.maximum((b * ns + i) * hb - 1, 0), col // width))

    def full(shape):
        return pl.BlockSpec(shape, lambda b, i: (0,) * len(shape))

    return pl.pallas_call(
        functools.partial(_seqmix_kernel, ts=ts),
        grid=(bsz, ns),
        in_specs=[cur(POOL_WIDTH, COL_POOL), halo(POOL_WIDTH, COL_POOL),
                  cur(2 * CONV_WIDTH, COL_CONV), halo(2 * CONV_WIDTH, COL_CONV),
                  full((4, POOL_GROUP, POOL_GROUP)), full((1, POOL_WIDTH)),
                  full((HALO, CONV_WIDTH)), full((1, CONV_WIDTH)), full((1, CONV_WIDTH)), full((1, CONV_WIDTH))],
        out_specs=[pl.BlockSpec((ts, POOL_WIDTH), lambda b, i: (b * ns + i, 0)),
                   pl.BlockSpec((ts, CONV_WIDTH), lambda b, i: (b * ns + i, 0))],
        out_shape=[jax.ShapeDtypeStruct((n, POOL_WIDTH), BF16),
                   jax.ShapeDtypeStruct((n, CONV_WIDTH), BF16)],
        scratch_shapes=[pltpu.VMEM((HALO + ts, POOL_WIDTH), F32),
                        pltpu.VMEM((HALO + ts, CONV_WIDTH), F32)],
        compiler_params=_cparams(("parallel", "arbitrary")),
        name="seqmix",
    )(proj, proj, proj, proj, pool_w, pool_scale, conv_w, conv_b, ln_g, ln_b)


def _rope_tables(positions, rot, period, base):
    lane = np.arange(LANES)
    r = lane % period - base
    active = (r >= 0) & (r < rot)
    f = np.where(active, r % (rot // 2), 0)
    sign = np.where(r < rot // 2, -1.0, 1.0).astype(np.float32)
    inv = jnp.power(ROPE_THETA, -jnp.arange(0, rot, 2, dtype=F32) / rot)
    ang = positions.astype(F32)[..., None] * inv
    cos = jnp.where(active, jnp.cos(ang)[..., f], 1.0)
    sin = jnp.where(active, jnp.sin(ang)[..., f] * sign, 0.0)
    return cos, sin


def _rope128(x, c, s, half, first):
    up = pltpu.roll(x, LANES - half, 1)
    down = pltpu.roll(x, half, 1)
    return x * c + jnp.where(first, up, down) * s


def _rmsnorm(x, g, width):
    ms = jnp.sum(x * x, axis=-1, keepdims=True) * (1.0 / width)
    return x * lax.rsqrt(ms + LN_EPS) * g


def _prep_kernel(cq_ref, ck_ref, cv_ref, mq_ref, mkv_ref, iq_ref, ikw_ref, mkr_ref,
                 cd_ref, sd_ref, ci_ref, si_ref, cm_ref, sm_ref,
                 qn_ref, kvn_ref, wuq_ref, wuk_ref, wuv_ref,
                 qd_ref, kd_ref, vd_ref, qi_ref, ki_ref, wi_ref, qm_ref, km_ref, vm_ref):
    lane = lax.broadcasted_iota(I32, (1, LANES), 1)

    cd, sd = cd_ref[0], sd_ref[0]
    first_d = (lane % DSA_HEAD_DIM) < DSA_ROT // 2
    for c in range(4):
        sl = slice(c * LANES, (c + 1) * LANES)
        qd_ref[0, :, sl] = (_rope128(cq_ref[:, sl], cd, sd, DSA_ROT // 2, first_d)
                            * DSA_HEAD_DIM ** -0.5).astype(BF16)
        kd_ref[0, :, sl] = _rope128(ck_ref[:, sl], cd, sd, DSA_ROT // 2, first_d).astype(BF16)
    vd_ref[0] = cv_ref[...].astype(BF16)

    ci, si = ci_ref[0], si_ref[0]
    first_i = (lane % IDX_DIM) < IDX_ROT // 2
    for c in range(2):
        sl = slice(c * LANES, (c + 1) * LANES)
        qi_ref[0, :, sl] = _rope128(iq_ref[:, sl], ci, si, IDX_ROT // 2, first_i).astype(BF16)
    ikw = ikw_ref[...]
    in_key = lane < IDX_DIM
    kr = _rope128(ikw, jnp.where(in_key, ci, 1.0), jnp.where(in_key, si, 0.0), IDX_ROT // 2, first_i)
    k32 = jnp.where(in_key, kr, 0.0)
    k64 = k32 + pltpu.roll(k32, IDX_DIM, 1)
    ki_ref[0] = (k64 + pltpu.roll(k64, 2 * IDX_DIM, 1)).astype(BF16)
    wi_ref[0] = pltpu.roll(ikw, LANES - IDX_DIM, 1) * (IDX_HEADS ** -0.5 * IDX_DIM ** -0.5)

    cm, sm = cm_ref[0], sm_ref[0]
    first_m = (lane >= MLA_NOPE) & (lane < MLA_NOPE + MLA_ROPE // 2)
    qn = _rmsnorm(mq_ref[...], qn_ref[...], MLA_Q_RANK).astype(BF16)
    q = _dot(qn, wuq_ref[...])
    for h in range(MLA_HEADS):
        sl = slice(h * LANES, (h + 1) * LANES)
        qm_ref[0, :, sl] = _rope128(q[:, sl], cm, sm, MLA_ROPE // 2, first_m).astype(BF16)
    c_kv = _rmsnorm(mkv_ref[...], kvn_ref[...], MLA_KV_RANK).astype(BF16)
    kn = _dot(c_kv, wuk_ref[...])
    vm_ref[0] = _dot(c_kv, wuv_ref[...]).astype(BF16)
    cmk = pltpu.roll(cm, LANES - MLA_NOPE, 1)
    smk = pltpu.roll(sm, LANES - MLA_NOPE, 1)
    krope = _rope128(mkr_ref[...], cmk, smk, MLA_ROPE // 2, lane < MLA_ROPE // 2)
    krope = pltpu.roll(jnp.where(lane < MLA_ROPE, krope, 0.0), MLA_NOPE, 1)
    for h in range(MLA_HEADS):
        sl = slice(h * LANES, (h + 1) * LANES)
        km_ref[0, :, sl] = (kn[:, sl] + krope).astype(BF16)


def _prep(proj, tables, bsz, seq, ts, qn, kvn, wuq, wuk, wuv):
    ns = seq // ts

    def cur(width, col):
        return pl.BlockSpec((ts, width), lambda b, i: (b * ns + i, col // width))

    def tab():
        return pl.BlockSpec((1, ts, LANES), lambda b, i: (b, i, 0))

    def full(shape):
        return pl.BlockSpec(shape, lambda b, i: (0,) * len(shape))

    def out(width):
        return pl.BlockSpec((1, ts, width), lambda b, i: (b, i, 0))

    widths = (512, 512, 512, 256, LANES, LANES, 1024, 1024, 512)
    dtypes = (BF16, BF16, BF16, BF16, BF16, F32, BF16, BF16, BF16)
    return pl.pallas_call(
        _prep_kernel,
        grid=(bsz, ns),
        in_specs=[cur(512, COL_CQ), cur(512, COL_CK), cur(512, COL_CV), cur(512, COL_MQ),
                  cur(256, COL_MKV), cur(256, COL_IQ), cur(LANES, COL_IKW), cur(LANES, COL_MKR)]
                 + [tab() for _ in range(6)]
                 + [full(qn.shape), full(kvn.shape), full(wuq.shape), full(wuk.shape), full(wuv.shape)],
        out_specs=[out(w) for w in widths],
        out_shape=[jax.ShapeDtypeStruct((bsz, seq, w), dt) for w, dt in zip(widths, dtypes)],
        compiler_params=_cparams(("parallel", "arbitrary")),
        name="prep",
    )(*([proj] * 8), *tables, qn, kvn, wuq, wuk, wuv)


def _softmax_step(s_ref, bias_ref, p_ref, a_ref, m_ref, l_ref, acc_ref, v, rc):
    rows, width = s_ref.shape
    for r0 in range(0, rows, rc):
        rs = slice(r0, r0 + rc)
        cols = []
        for c0 in range(0, width, LANES):
            sc = s_ref[rs, c0:c0 + LANES]
            if bias_ref is not None:
                rb = r0 % bias_ref.shape[0]
                sc = sc + bias_ref[rb:rb + rc, c0:c0 + LANES]
            cols.append(sc)
        cmax = functools.reduce(jnp.maximum, cols)
        m_old = m_ref[rs]
        m_new = jnp.maximum(m_old, jnp.max(cmax, axis=1, keepdims=True))
        alpha = jnp.exp(m_old - m_new)
        psum = None
        for j, sc in enumerate(cols):
            pc = jnp.exp(sc - m_new)
            p_ref[rs, j * LANES:(j + 1) * LANES] = pc.astype(BF16)
            psum = pc if psum is None else psum + pc
        l_ref[rs] = alpha * l_ref[rs] + jnp.sum(psum, axis=1, keepdims=True)
        m_ref[rs] = m_new
        a_ref[rs] = alpha
    acc_ref[...] = a_ref[...] * acc_ref[...] + _dot(p_ref[...], v)


def _lane_max_i32(x):
    sh = LANES // 2
    while sh >= 1:
        x = jnp.maximum(x, pltpu.roll(x, sh, 1))
        sh //= 2
    return x


def _dsa_kernel(q_ref, k_ref, v_ref, qi_ref, ki_ref, w_ref, o_ref, key_ref, qs_ref, qh_ref, bias_ref, s_ref,
                p_ref, a_ref, st_ref, *, tq, tks, n_sel):
    i = pl.program_id(1)
    nkt = ((i + 1) * tq + tks - 1) // tks
    row = i * tq + lax.broadcasted_iota(I32, (tq, 1), 0)
    lane = lax.broadcasted_iota(I32, (1, LANES), 1)

    for h in range(IDX_HEADS):
        qi = qi_ref[0, :, (h // 4) * LANES:(h // 4 + 1) * LANES]
        qs_ref[h * tq:(h + 1) * tq, :] = jnp.where(lane // IDX_DIM == h % 4, qi, jnp.zeros_like(qi))
    for h in range(DSA_HEADS):
        qp = q_ref[0, :, (h // 2) * LANES:(h // 2 + 1) * LANES]
        qh_ref[h // 2, (h % 2) * tq:(h % 2 + 1) * tq, :] = jnp.where(lane // DSA_HEAD_DIM == h % 2, qp,
                                                                     jnp.zeros_like(qp))

    w = w_ref[0]
    w_h = [w[:, h:h + 1] for h in range(IDX_HEADS)]
    hg = 4

    def score_tile(kt, carry):
        k0 = pl.multiple_of(kt * tks, tks)
        kit = ki_ref[0, pl.ds(k0, tks), :]
        acc = jnp.zeros((tq, tks), F32)
        for g in range(IDX_HEADS // hg):
            res = _dot_nt(qs_ref[g * hg * tq:(g + 1) * hg * tq, :], kit)
            for j in range(hg):
                acc = acc + w_h[g * hg + j] * jnp.maximum(res[j * tq:(j + 1) * tq, :], 0.0)
        bits = lax.bitcast_convert_type(acc, I32)
        key = jnp.where(bits >= 0, bits, bits ^ 0x7FFFFFFF)
        col = k0 + lax.broadcasted_iota(I32, (1, tks), 1)
        key_ref[:, pl.ds(k0, tks)] = jnp.where(col <= row, key, INT_MIN)
        return carry

    lax.fori_loop(0, nkt, score_tile, 0)

    def sweep(fn, init):
        def body(kt, part):
            k0 = pl.multiple_of(kt * tks, tks)
            out = []
            for r0 in range(0, tq, 64):
                pr = part[r0:r0 + 64]
                for c in range(tks // LANES):
                    c0 = k0 + c * LANES
                    pr = fn(pr, key_ref[r0:r0 + 64, pl.ds(c0, LANES)], slice(r0, r0 + 64), c0)
                out.append(pr)
            return jnp.concatenate(out, axis=0)
        return lax.fori_loop(0, nkt, body, init)

    def count_if(pred):
        part = sweep(lambda pr, kk, rs, c0: pr + jnp.where(pred(kk, rs, c0), 1.0, 0.0), jnp.zeros((tq, LANES), F32))
        return jnp.sum(part, axis=1, keepdims=True)

    def count_ge(cand):
        return count_if(lambda kk, rs, c0: kk >= cand[rs])

    kmax = _lane_max_i32(sweep(lambda pr, kk, rs, c0: jnp.maximum(pr, kk), jnp.full((tq, LANES), INT_MIN, I32)))

    def search_cond(st):
        return jnp.logical_and(st[-1] > 0, st[0] < 80)

    def search_body(st):
        it, lo, hi, c_lo, _ = st
        mid = (lo >> 1) + (hi >> 1) + (lo & hi & 1)
        stride = jnp.left_shift(jnp.int32(1), jnp.minimum(23 + 2 * it, 30))
        top = jnp.where(hi >= INT_MIN + 1 + stride, hi - stride, INT_MIN + 1)
        probe = jnp.maximum(mid, top)
        c = count_ge(probe)
        ge = c >= n_sel
        lo_n = jnp.where(ge, probe, lo)
        hi_n = jnp.where(ge, jnp.where(c == n_sel, probe + 1, hi), probe)
        active = jnp.max(jnp.where(hi_n - lo_n != 1, 1, 0))
        return it + 1, lo_n, hi_n, jnp.where(ge, c, c_lo), active

    lo0 = jnp.full((tq, LANES), INT_MIN + 1, I32)
    _, thr, _, c_thr, _ = lax.while_loop(
        search_cond, search_body, (jnp.int32(0), lo0, kmax + 1, (row + 1).astype(F32), jnp.int32(1)))

    @pl.when(jnp.max(jnp.where(c_thr > n_sel, 1, 0)) > 0)
    def _():
        need = n_sel - count_ge(thr + 1)

        def tied_upto(j):
            return count_if(lambda kk, rs, c0: jnp.logical_and(kk == thr[rs], c0 + lane <= j[rs]))

        def col_body(_, st):
            j_lo, j_hi = st
            mid = (j_lo + j_hi) >> 1
            ok = tied_upto(mid) >= need
            return jnp.where(ok, j_lo, mid), jnp.where(ok, mid, j_hi)

        j0 = (jnp.full((tq, LANES), -1, I32), jnp.full((tq, LANES), 0, I32) + (nkt * tks - 1))
        _, last = lax.fori_loop(0, key_ref.shape[1].bit_length(), col_body, j0)

        def demote(kt, carry):
            k0 = pl.multiple_of(kt * tks, tks)
            for r0 in range(0, tq, 64):
                rs = slice(r0, r0 + 64)
                for c in range(tks // LANES):
                    c0 = k0 + c * LANES
                    kk = key_ref[rs, pl.ds(c0, LANES)]
                    drop = jnp.logical_and(kk == thr[rs], c0 + lane > last[rs])
                    key_ref[rs, pl.ds(c0, LANES)] = jnp.where(drop, thr[rs] - 1, kk)
            return carry

        lax.fori_loop(0, nkt, demote, 0)

    npair = DSA_HEADS // 2
    st_ref[:, 0] = jnp.full((npair, 2 * tq, LANES), NEG_BIG, F32)
    st_ref[:, 1] = jnp.zeros((npair, 2 * tq, LANES), F32)
    st_ref[:, 2] = jnp.zeros((npair, 2 * tq, LANES), F32)

    def att_tile(kt, carry):
        k0 = pl.multiple_of(kt * tks, tks)
        for r0 in range(0, tq, 64):
            for c in range(tks // LANES):
                kk = key_ref[r0:r0 + 64, pl.ds(k0 + c * LANES, LANES)]
                bias_ref[r0:r0 + 64, c * LANES:(c + 1) * LANES] = jnp.where(kk >= thr[r0:r0 + 64], 0.0, NEG_BIG)
        for p in range(npair):
            cols = slice(p * LANES, (p + 1) * LANES)
            s_ref[p] = _dot_nt(qh_ref[p], k_ref[0, pl.ds(k0, tks), cols])
            _softmax_step(s_ref.at[p], bias_ref, p_ref.at[p], a_ref.at[p], st_ref.at[p, 0], st_ref.at[p, 1],
                          st_ref.at[p, 2], v_ref[0, pl.ds(k0, tks), cols], 64)
        return carry

    lax.fori_loop(0, nkt, att_tile, 0)
    for p in range(npair):
        o = st_ref[p, 2] / st_ref[p, 1]
        o_ref[0, :, p * LANES:(p + 1) * LANES] = jnp.where(lane < DSA_HEAD_DIM, o[:tq], o[tq:]).astype(BF16)


def _dsa(qd, kd, vd, qi, ki, wi, tq, tks):
    bsz, seq, _ = qd.shape
    n_sel = min(DSA_TOPK, seq // 4)
    one = pl.Buffered(1)
    npair = DSA_HEADS // 2

    def blk(width):
        return pl.BlockSpec((1, tq, width), lambda b, i: (b, i, 0))

    def res(width):
        return pl.BlockSpec((1, seq, width), lambda b, i: (b, 0, 0), pipeline_mode=one)

    return pl.pallas_call(
        functools.partial(_dsa_kernel, tq=tq, tks=tks, n_sel=n_sel),
        grid=(bsz, seq // tq),
        in_specs=[blk(512), res(512), res(512), blk(256), res(LANES), blk(LANES)],
        out_specs=blk(512),
        out_shape=jax.ShapeDtypeStruct((bsz, seq, 512), BF16),
        scratch_shapes=[pltpu.VMEM((tq, seq), I32),
                        pltpu.VMEM((IDX_HEADS * tq, LANES), BF16),
                        pltpu.VMEM((npair, 2 * tq, LANES), BF16),
                        pltpu.VMEM((tq, tks), F32),
                        pltpu.VMEM((npair, 2 * tq, tks), F32),
                        pltpu.VMEM((npair, 2 * tq, tks), BF16),
                        pltpu.VMEM((npair, 2 * tq, LANES), F32),
                        pltpu.VMEM((npair, 3, 2 * tq, LANES), F32)],
        compiler_params=_cparams(("parallel", "arbitrary")),
        name="dsa",
    )(qd, kd, vd, qi, ki, wi)


def _mla_kernel(q_ref, k_ref, v_ref, o_ref, bias_ref, s_ref, p_ref, a_ref, st_ref, *, tq):
    i = pl.program_id(2)
    scale = (MLA_NOPE + MLA_ROPE) ** -0.5
    lane = lax.broadcasted_iota(I32, (1, LANES), 1)
    st_ref[:, 0] = jnp.full((2, tq, LANES), NEG_BIG, F32)
    st_ref[:, 1] = jnp.zeros((2, tq, LANES), F32)
    st_ref[:, 2] = jnp.zeros((2, tq, LANES), F32)

    def step(k0, diag):
        vt = v_ref[0, pl.ds(k0, tq), :]
        for hh in range(2):
            cols = slice(hh * LANES, (hh + 1) * LANES)
            s_ref[hh] = _dot_nt(q_ref[0, :, cols], k_ref[0, pl.ds(k0, tq), cols]) * scale
            _softmax_step(s_ref.at[hh], bias_ref if diag else None, p_ref.at[hh], a_ref.at[hh], st_ref.at[hh, 0],
                          st_ref.at[hh, 1], st_ref.at[hh, 2], vt, 64)

    def full_tile(kt, carry):
        step(pl.multiple_of(kt * tq, tq), False)
        return carry

    lax.fori_loop(0, i, full_tile, 0)
    rowi = lax.broadcasted_iota(I32, (tq, tq), 0)
    coli = lax.broadcasted_iota(I32, (tq, tq), 1)
    bias_ref[...] = jnp.where(coli <= rowi, 0.0, NEG_BIG)
    step(pl.multiple_of(i * tq, tq), True)
    o_ref[0] = jnp.where(lane < MLA_V, st_ref[0, 2] / st_ref[0, 1], st_ref[1, 2] / st_ref[1, 1]).astype(BF16)


def _mla(qm, km, vm, tq):
    bsz, seq, _ = qm.shape
    return pl.pallas_call(
        functools.partial(_mla_kernel, tq=tq),
        grid=(bsz, MLA_HEADS // 2, seq // tq),
        in_specs=[pl.BlockSpec((1, tq, 2 * LANES), lambda b, p, i: (b, i, p)),
                  pl.BlockSpec((1, seq, 2 * LANES), lambda b, p, i: (b, 0, p)),
                  pl.BlockSpec((1, seq, LANES), lambda b, p, i: (b, 0, p))],
        out_specs=pl.BlockSpec((1, tq, LANES), lambda b, p, i: (b, i, p)),
        out_shape=jax.ShapeDtypeStruct((bsz, seq, MLA_HEADS * MLA_V), BF16),
        scratch_shapes=[pltpu.VMEM((tq, tq), F32),
                        pltpu.VMEM((2, tq, tq), F32),
                        pltpu.VMEM((2, tq, tq), BF16),
                        pltpu.VMEM((2, tq, LANES), F32),
                        pltpu.VMEM((2, 3, tq, LANES), F32)],
        compiler_params=_cparams(("parallel", "parallel", "arbitrary")),
        name="mla",
    )(qm, km, vm)


def _merge_kernel(a_ref, b_ref, c_ref, d_ref, g_ref, x_ref, wa_ref, wb_ref, wc_ref, wd_ref, wo_ref,
                  lg_ref, lb_ref, h_ref):
    merged = None
    for j, (br, w) in enumerate(((a_ref, wa_ref), (b_ref, wb_ref), (c_ref, wc_ref), (d_ref, wd_ref))):
        y = jax.nn.sigmoid(g_ref[:, j * D_MODEL:(j + 1) * D_MODEL]) * _dot(br[...], w[...])
        merged = y if merged is None else merged + y
    out = _dot(merged.astype(BF16), wo_ref[...])
    h_ref[...] = _layernorm(DEEPNORM_ALPHA * x_ref[...] + out, lg_ref[...], lb_ref[...])


def _merge(ya, yb, yc, yd, proj, x, wa, wb, wc, wd, wo, lg, lb, tm):
    n = x.shape[0]

    def rows(width):
        return pl.BlockSpec((tm, width), lambda i: (i, 0))

    def full(shape):
        return pl.BlockSpec(shape, lambda i: (0,) * len(shape))

    return pl.pallas_call(
        _merge_kernel,
        grid=(n // tm,),
        in_specs=[rows(512), rows(512), rows(512), rows(512), rows(N_BRANCH * D_MODEL), rows(D_MODEL),
                  full(wa.shape), full(wb.shape), full(wc.shape), full(wd.shape), full(wo.shape),
                  full(lg.shape), full(lb.shape)],
        out_specs=rows(D_MODEL),
        out_shape=jax.ShapeDtypeStruct((n, D_MODEL), F32),
        compiler_params=_cparams(("parallel",)),
        name="merge",
    )(ya, yb, yc, yd, proj, x, wa, wb, wc, wd, wo, lg, lb)


def _router_kernel(h_ref, wh_ref, wl_ref, b_ref, eidx_ref, gate_ref, rank_ref, cnt_ref, carry_ref, *, tm):
    i = pl.program_id(0)

    @pl.when(i == 0)
    def _():
        carry_ref[...] = jnp.zeros_like(carry_ref)

    h = h_ref[...]
    hh = h.astype(BF16)
    hl = (h - hh.astype(F32)).astype(BF16)
    wh = wh_ref[...]
    logits = _dot(hh, wh) + _dot(hl, wh) + _dot(hh, wl_ref[...]) + b_ref[...]
    lane = lax.broadcasted_iota(I32, (1, LANES), 1)
    lane_f = lane.astype(F32)
    vals, idxs = [], []
    cur = logits
    for _ in range(TOP_K):
        m = jnp.max(cur, axis=1, keepdims=True)
        ix = jnp.min(jnp.where(cur == m, lane_f, float(LANES)), axis=1, keepdims=True)
        vals.append(m)
        idxs.append(ix)
        cur = jnp.where(lane_f == ix, -jnp.inf, cur)
    ex = [jnp.exp(v - vals[0]) for v in vals]
    den = ex[0] + ex[1] + ex[2] + ex[3]
    hot = [lane_f == ix for ix in idxs]
    cnt_tok = sum(jnp.where(hk, 1.0, 0.0) for hk in hot)
    r_i = lax.broadcasted_iota(I32, (tm, tm), 0)
    c_i = lax.broadcasted_iota(I32, (tm, tm), 1)
    lower = jnp.where(c_i < r_i, 1.0, 0.0).astype(BF16)
    prefix = _dot(lower, cnt_tok.astype(BF16)) + carry_ref[...]
    eidx = jnp.zeros((tm, LANES), F32)
    gate = jnp.zeros((tm, LANES), F32)
    rank = jnp.zeros((tm, LANES), F32)
    for k in range(TOP_K):
        rk = jnp.sum(jnp.where(hot[k], prefix, 0.0), axis=1, keepdims=True)
        eidx = jnp.where(lane == k, idxs[k], eidx)
        gate = jnp.where(lane == k, ex[k] / den, gate)
        rank = jnp.where(lane == k, rk, rank)
    eidx_ref[...] = eidx.astype(I32)
    gate_ref[...] = gate
    rank_ref[...] = rank.astype(I32)
    carry_ref[...] = carry_ref[...] + jnp.sum(cnt_tok, axis=0, keepdims=True)
    cnt_ref[...] = carry_ref[...].astype(I32)


def _router(h, wh, wl, b, tm):
    n = h.shape[0]

    def rows(width):
        return pl.BlockSpec((tm, width), lambda i: (i, 0))

    def full(shape):
        return pl.BlockSpec(shape, lambda i: (0,) * len(shape))

    return pl.pallas_call(
        functools.partial(_router_kernel, tm=tm),
        grid=(n // tm,),
        in_specs=[rows(D_MODEL), full(wh.shape), full(wl.shape), full(b.shape)],
        out_specs=[rows(LANES), rows(LANES), rows(LANES), full((1, LANES))],
        out_shape=[jax.ShapeDtypeStruct((n, LANES), I32), jax.ShapeDtypeStruct((n, LANES), F32),
                   jax.ShapeDtypeStruct((n, LANES), I32), jax.ShapeDtypeStruct((1, LANES), I32)],
        scratch_shapes=[pltpu.VMEM((1, LANES), F32)],
        compiler_params=_cparams(("arbitrary",)),
        name="router",
    )(h, wh, wl, b)


def _row_copy(src_hbm, row, dst, r, sem):
    return pltpu.make_async_copy(src_hbm.at[pl.ds(row, 1), :], dst.at[pl.ds(r, 1), :], sem)


def _gather_rows(idx_vmem, idx_smem, isem, src_hbm, dst, sem, nrows, inline):
    cp = pltpu.make_async_copy(idx_vmem.at[0, 0], idx_smem, isem)
    cp.start()
    cp.wait()
    if inline:
        for r in range(nrows):
            _row_copy(src_hbm, idx_smem[r], dst, r, sem).start()
        return

    def body(r, carry):
        _row_copy(src_hbm, idx_smem[r], dst, r, sem).start()
        return carry

    lax.fori_loop(0, nrows, body, 0, unroll=8)


def _wait_rows(src_hbm, dst, sem, nrows):
    pltpu.make_async_copy(src_hbm.at[pl.ds(0, nrows), :], dst, sem).wait()


def _expert_kernel(be_ref, idx0_ref, idxn_ref, h_hbm, wgu_ref, bgu_ref, wd_ref, bd_ref, y_ref,
                   xbuf0, xbuf1, idx_smem, sem, isem, *, rb, nb):
    del be_ref
    i = pl.program_id(0)
    slot = i % 2
    xbufs = (xbuf0, xbuf1)

    @pl.when(i == 0)
    def _():
        _gather_rows(idx0_ref, idx_smem, isem, h_hbm, xbuf0, sem.at[0], rb, False)

    def work(cur):
        _wait_rows(h_hbm, xbufs[cur], sem.at[cur], rb)
        _gather_rows(idxn_ref, idx_smem, isem, h_hbm, xbufs[1 - cur], sem.at[1 - cur], rb, True)
        xb = xbufs[cur][...].astype(BF16)
        fc = 512
        y = None
        for c in range(D_FF // fc):
            g = _dot(xb, wgu_ref[0, :, c * fc:(c + 1) * fc]) + bgu_ref[0, :, c * fc:(c + 1) * fc]
            lin = (_dot(xb, wgu_ref[0, :, D_FF + c * fc:D_FF + (c + 1) * fc])
                   + bgu_ref[0, :, D_FF + c * fc:D_FF + (c + 1) * fc])
            g = jnp.minimum(g, SWIGLU_LIMIT)
            lin = jnp.clip(lin, -SWIGLU_LIMIT, SWIGLU_LIMIT)
            act = (g * jax.nn.sigmoid(SWIGLU_ALPHA * g) * (lin + 1.0)).astype(BF16)
            part = _dot(act, wd_ref[0, c * fc:(c + 1) * fc, :])
            y = part if y is None else y + part
        y_ref[...] = y + bd_ref[0]

    for cur in range(2):
        pl.when(slot == cur)(functools.partial(work, cur))

    for cur in range(2):
        pl.when(jnp.logical_and(i == nb - 1, slot == cur))(
            functools.partial(_wait_rows, h_hbm, xbufs[1 - cur], sem.at[1 - cur], rb))


def _experts(blk_expert, row_src, h, wgu, bgu, wd, bd, rb):
    nb = blk_expert.shape[0]
    grid_spec = pltpu.PrefetchScalarGridSpec(
        num_scalar_prefetch=1,
        grid=(nb,),
        in_specs=[pl.BlockSpec((1, 1, rb), lambda i, be: (0, 0, 0)),
                  pl.BlockSpec((1, 1, rb), lambda i, be: (jnp.minimum(i + 1, nb - 1), 0, 0)),
                  pl.BlockSpec(memory_space=pl.ANY),
                  pl.BlockSpec((1, D_MODEL, 2 * D_FF), lambda i, be: (be[i], 0, 0)),
                  pl.BlockSpec((1, 1, 2 * D_FF), lambda i, be: (be[i], 0, 0)),
                  pl.BlockSpec((1, D_FF, D_MODEL), lambda i, be: (be[i], 0, 0)),
                  pl.BlockSpec((1, 1, D_MODEL), lambda i, be: (be[i], 0, 0))],
        out_specs=pl.BlockSpec((rb, D_MODEL), lambda i, be: (i, 0)),
        scratch_shapes=[pltpu.VMEM((rb, D_MODEL), F32),
                        pltpu.VMEM((rb, D_MODEL), F32),
                        pltpu.SMEM((rb,), I32),
                        pltpu.SemaphoreType.DMA((2,)),
                        pltpu.SemaphoreType.DMA(())],
    )
    return pl.pallas_call(
        functools.partial(_expert_kernel, rb=rb, nb=nb),
        grid_spec=grid_spec,
        out_shape=jax.ShapeDtypeStruct((nb * rb, D_MODEL), F32),
        compiler_params=_cparams(("arbitrary",)),
        name="experts",
    )(blk_expert, row_src, row_src, h, wgu, bgu, wd, bd)


def _combine_kernel(idx0_ref, idxn_ref, y_hbm, gate_ref, h_ref, lg_ref, lb_ref, x_ref, xb_ref,
                    ybuf, idx_smem, sem, isem, *, tm, nt):
    i = pl.program_id(0)
    slot = i % 2

    @pl.when(i == 0)
    def _():
        _gather_rows(idx0_ref, idx_smem, isem, y_hbm, ybuf.at[0], sem.at[0], TOP_K * tm, False)

    _wait_rows(y_hbm, ybuf.at[slot], sem.at[slot], TOP_K * tm)
    _gather_rows(idxn_ref, idx_smem, isem, y_hbm, ybuf.at[1 - slot], sem.at[1 - slot], TOP_K * tm, True)
    gate = gate_ref[...]
    moe = None
    for k in range(TOP_K):
        t = ybuf[slot, k * tm:(k + 1) * tm, :] * gate[:, k:k + 1]
        moe = t if moe is None else moe + t
    out = _layernorm(DEEPNORM_ALPHA * h_ref[...] + moe, lg_ref[...], lb_ref[...])
    x_ref[...] = out
    xb_ref[...] = out.astype(BF16)

    @pl.when(i == nt - 1)
    def _():
        _wait_rows(y_hbm, ybuf.at[1 - slot], sem.at[1 - slot], TOP_K * tm)


def _combine(dest_t, ybuf, gate, h, lg, lb, tm):
    n = h.shape[0]
    nt = n // tm

    def rows(width):
        return pl.BlockSpec((tm, width), lambda i: (i, 0))

    def full(shape):
        return pl.BlockSpec(shape, lambda i: (0,) * len(shape))

    return pl.pallas_call(
        functools.partial(_combine_kernel, tm=tm, nt=nt),
        grid=(nt,),
        in_specs=[pl.BlockSpec((1, 1, TOP_K * tm), lambda i: (0, 0, 0)),
                  pl.BlockSpec((1, 1, TOP_K * tm), lambda i: (jnp.minimum(i + 1, nt - 1), 0, 0)),
                  pl.BlockSpec(memory_space=pl.ANY),
                  rows(LANES), rows(D_MODEL), full(lg.shape), full(lb.shape)],
        out_specs=[rows(D_MODEL), rows(D_MODEL)],
        out_shape=[jax.ShapeDtypeStruct((n, D_MODEL), F32), jax.ShapeDtypeStruct((n, D_MODEL), BF16)],
        scratch_shapes=[pltpu.VMEM((2, TOP_K * tm, D_MODEL), F32),
                        pltpu.SMEM((TOP_K * tm,), I32),
                        pltpu.SemaphoreType.DMA((2,)),
                        pltpu.SemaphoreType.DMA(())],
        compiler_params=_cparams(("arbitrary",)),
        name="combine",
    )(dest_t, dest_t, ybuf, gate, h, lg, lb)


def _pad_cols(w, width):
    return jnp.pad(w, ((0, 0), (0, width - w.shape[1])))


def _pack_w_in(w_in):
    offs = np.cumsum((0,) + SPLITS)
    seg = [w_in[:, int(offs[j]):int(offs[j + 1])] for j in range(len(SPLITS))]
    u_pool, u_conv, c_q, c_k, c_v, i_q, i_k, i_w, m_q, m_kv, m_kr, g_all = seg
    parts = [g_all, u_conv, u_pool, c_q, c_k, c_v, _pad_cols(m_q, 512), m_kv, i_q,
             _pad_cols(jnp.concatenate([i_k, i_w], axis=1), LANES), _pad_cols(m_kr, LANES)]
    return jnp.concatenate(parts, axis=1).astype(BF16)


def _pack_heads(w, widths_in, total):
    r = w.shape[0]
    per = sum(widths_in)
    w = w.reshape(r, -1, per)
    w = jnp.pad(w, ((0, 0), (0, 0), (0, total - per)))
    return w.reshape(r, -1)


def _layer(x, xb, tables, bsz, seq, cfg, w_in, pool_w, pool_scale, pool_out, conv_w, conv_b, conv_ln_g,
           conv_ln_b, conv_out, dsa_out, mla_q_norm, mla_kv_norm, mla_wuq, mla_wuk, mla_wuv, mla_out, w_o,
           ln1_g, ln1_b, router_w, router_b, exp_w_gu, exp_b_gu, exp_w_d, exp_b_d, ln2_g, ln2_b):
    n = bsz * seq
    row = lambda v: v.reshape(1, -1).astype(F32)

    proj = _inproj(xb, _pack_w_in(w_in), cfg["tm_in"], cfg["tn_in"])

    ya, yb = _seqmix(proj, bsz, seq, cfg["ts"], pool_w.astype(BF16), row(pool_scale),
                     jnp.pad(conv_w, ((0, HALO - CONV_K), (0, 0))), row(conv_b), row(conv_ln_g), row(conv_ln_b))

    wuq = jnp.pad(_pack_heads(mla_wuq, (MLA_NOPE, MLA_ROPE), LANES), ((0, 512 - MLA_Q_RANK), (0, 0))).astype(BF16)
    wuk = _pack_heads(mla_wuk, (MLA_NOPE,), LANES).astype(BF16)
    qd, kd, vd, qi, ki, wi, qm, km, vm = _prep(
        proj, tables, bsz, seq, cfg["ts"], _pad_cols(row(mla_q_norm), 512), row(mla_kv_norm),
        wuq, wuk, mla_wuv.astype(BF16))

    yc = _dsa(qd, kd, vd, qi, ki, wi, cfg["tq_dsa"], cfg["tk_dsa"]).reshape(n, -1)
    yd = _mla(qm, km, vm, cfg["tq_mla"]).reshape(n, -1)

    h = _merge(ya, yb, yc, yd, proj, x, pool_out.astype(BF16), conv_out.astype(BF16), dsa_out.astype(BF16),
               mla_out.astype(BF16), w_o.astype(BF16), row(ln1_g), row(ln1_b), cfg["tm_merge"])

    rw = _pad_cols(router_w, LANES)
    rwh = rw.astype(BF16)
    rwl = (rw - rwh.astype(F32)).astype(BF16)
    rb_ = jnp.pad(row(router_b), ((0, 0), (0, LANES - N_EXPERTS)), constant_values=NEG_BIG)
    eidx, gate, rank, counts = _router(h, rwh, rwl, rb_, cfg["tm_router"])

    counts = counts[0, :N_EXPERTS]
    padded = (counts + MOE_BLOCK - 1) // MOE_BLOCK * MOE_BLOCK
    pad_end = jnp.cumsum(padded)
    pad_start = pad_end - padded
    n_blocks = -(-(n * TOP_K) // MOE_BLOCK) + N_EXPERTS
    dest = pad_start[eidx[:, :TOP_K]] + rank[:, :TOP_K]
    tok = jnp.broadcast_to(jnp.arange(n, dtype=I32)[:, None], (n, TOP_K))
    row_src = jnp.zeros((n_blocks * MOE_BLOCK,), I32).at[dest.reshape(-1)].set(tok.reshape(-1), unique_indices=True)
    blk_start = jnp.arange(n_blocks, dtype=I32) * MOE_BLOCK
    blk_expert = jnp.minimum(jnp.sum((pad_end[None, :] <= blk_start[:, None]).astype(I32), axis=1), N_EXPERTS - 1)

    ybuf = _experts(blk_expert, row_src.reshape(n_blocks, 1, MOE_BLOCK), h,
                    exp_w_gu.astype(BF16), exp_b_gu.reshape(N_EXPERTS, 1, -1),
                    exp_w_d.astype(BF16), exp_b_d.reshape(N_EXPERTS, 1, -1), MOE_BLOCK)

    tmc = cfg["tm_comb"]
    dest_t = dest.reshape(n // tmc, tmc, TOP_K).transpose(0, 2, 1).reshape(n // tmc, 1, TOP_K * tmc)
    return _combine(dest_t, ybuf, gate, h, row(ln2_g), row(ln2_b), tmc)


_CFG = dict(tm_in=1024, tn_in=768, ts=512, tq_dsa=256, tk_dsa=512, tq_mla=512, tm_merge=256,
            tm_router=512, tm_comb=128)


def _forward(cfg, x, positions, *weights):
    bsz, seq, d = x.shape
    n = bsz * seq
    tables = (*_rope_tables(positions, DSA_ROT, DSA_HEAD_DIM, 0),
              *_rope_tables(positions, IDX_ROT, IDX_DIM, 0),
              *_rope_tables(positions, MLA_ROPE, LANES, MLA_NOPE))
    xf = x.reshape(n, d)
    xb = xf.astype(BF16)
    for l in range(DEPTH):
        xf, xb = _layer(xf, xb, tables, bsz, seq, cfg, *[w[l] for w in weights])
    return xf.reshape(bsz, seq, d)


def kernel(x, positions, w_in, pool_w, pool_scale, pool_out, conv_w, conv_b, conv_ln_g, conv_ln_b, conv_out,
           dsa_out, mla_q_norm, mla_kv_norm, mla_wuq, mla_wuk, mla_wuv, mla_out, w_o, ln1_g, ln1_b, router_w,
           router_b, exp_w_gu, exp_b_gu, exp_w_d, exp_b_d, ln2_g, ln2_b):
    return _forward(_CFG, x, positions, w_in, pool_w, pool_scale, pool_out, conv_w, conv_b, conv_ln_g,
                    conv_ln_b, conv_out, dsa_out, mla_q_norm, mla_kv_norm, mla_wuq, mla_wuk, mla_wuv, mla_out,
                    w_o, ln1_g, ln1_b, router_w, router_b, exp_w_gu, exp_b_gu, exp_w_d, exp_b_d, ln2_g, ln2_b)
```

```python
import functools

import numpy as np
import jax
import jax.numpy as jnp
from jax import lax
from jax.experimental import pallas as pl
from jax.experimental.pallas import tpu as pltpu

F32 = jnp.float32
BF16 = jnp.bfloat16
I32 = jnp.int32

D_MODEL = 1024
DEPTH = 2
ROPE_THETA = 500000.0
LN_EPS = 1e-5
POOL_WIDTH = 512
POOL_WINDOWS = (2, 4, 8, 16)
POOL_GROUP = POOL_WIDTH // 4
CONV_WIDTH = 512
CONV_K = 31
DSA_HEADS = 8
DSA_HEAD_DIM = 64
DSA_ROT = DSA_HEAD_DIM // 4
IDX_HEADS = 8
IDX_DIM = 32
IDX_ROT = IDX_DIM // 4
DSA_TOPK = 256
MLA_HEADS = 8
MLA_NOPE = 64
MLA_ROPE = 32
MLA_V = 64
MLA_Q_RANK = 384
MLA_KV_RANK = 256
N_BRANCH = 4
N_EXPERTS = 32
TOP_K = 4
D_FF = 1024
SWIGLU_ALPHA = 1.702
SWIGLU_LIMIT = 7.0
MOE_BLOCK = 512
DEEPNORM_ALPHA = (2 * DEPTH) ** 0.25
SPLITS = (POOL_WIDTH, 2 * CONV_WIDTH,
          DSA_HEADS * DSA_HEAD_DIM, DSA_HEADS * DSA_HEAD_DIM, DSA_HEADS * DSA_HEAD_DIM,
          IDX_HEADS * IDX_DIM, IDX_DIM, IDX_HEADS,
          MLA_Q_RANK, MLA_KV_RANK, MLA_ROPE,
          N_BRANCH * D_MODEL)

LANES = 128
HALO = 32
VMEM_LIMIT = 56 * 1024 * 1024

COL_GATE = 0
COL_CONV = 4096
COL_POOL = 5120
COL_CQ = 5632
COL_CK = 6144
COL_CV = 6656
COL_MQ = 7168
COL_MKV = 7680
COL_IQ = 7936
COL_IKW = 8192
COL_MKR = 8320
P_COLS = 8448

INT_MIN = -(2 ** 31)
NEG_BIG = -1e30


def _cparams(sem, vmem=VMEM_LIMIT):
    return pltpu.CompilerParams(dimension_semantics=sem, vmem_limit_bytes=vmem)


def _dot(a, b):
    return jnp.dot(a, b, preferred_element_type=F32)


def _dot_nt(a, b):
    return lax.dot_general(a, b, (((1,), (1,)), ((), ())), preferred_element_type=F32)


def _layernorm(z, g, b):
    mu = jnp.mean(z, axis=-1, keepdims=True)
    zc = z - mu
    var = jnp.mean(zc * zc, axis=-1, keepdims=True)
    return zc * lax.rsqrt(var + LN_EPS) * g + b


def _matmul_kernel(x_ref, w_ref, o_ref):
    o_ref[...] = _dot(x_ref[...], w_ref[...])


def _inproj(xb, wp, tm, tn):
    n, d = xb.shape
    p = wp.shape[1]
    return pl.pallas_call(
        _matmul_kernel,
        grid=(n // tm, p // tn),
        in_specs=[pl.BlockSpec((tm, d), lambda i, j: (i, 0)),
                  pl.BlockSpec((d, tn), lambda i, j: (0, j))],
        out_specs=pl.BlockSpec((tm, tn), lambda i, j: (i, j)),
        out_shape=jax.ShapeDtypeStruct((n, p), F32),
        compiler_params=_cparams(("parallel", "arbitrary")),
        name="inproj",
    )(xb, wp)


def _seqmix_kernel(up_ref, uph_ref, uc_ref, uch_ref, pw_ref, ps_ref, cw_ref, cb_ref, lg_ref, lb_ref,
                   a_ref, b_ref, pext, hext, *, ts):
    i = pl.program_id(1)
    has_prev = i > 0

    cur = up_ref[...]
    pext[0:HALO, :] = jnp.where(has_prev, uph_ref[...], 0.0)
    pext[HALO:HALO + ts, :] = cur
    pos = i * ts + lax.broadcasted_iota(I32, (ts, 1), 0)
    for gi, w in enumerate(POOL_WINDOWS):
        cols = slice(gi * POOL_GROUP, (gi + 1) * POOL_GROUP)
        acc = cur[:, cols]
        for d in range(1, w):
            acc = acc + pext[HALO - d:HALO - d + ts, cols]
        cnt = jnp.minimum(pos + 1, w).astype(F32)
        pooled = acc / cnt - cur[:, cols]
        mixed = _dot(pooled.astype(BF16), pw_ref[gi])
        a_ref[:, cols] = (mixed * ps_ref[:, cols]).astype(BF16)

    def glu(u):
        return u[:, :CONV_WIDTH] * jax.nn.sigmoid(u[:, CONV_WIDTH:])

    hext[0:HALO, :] = jnp.where(has_prev, glu(uch_ref[...]), 0.0)
    hext[HALO:HALO + ts, :] = glu(uc_ref[...])
    rc = 64
    for c in range(ts // rc):
        base = HALO + c * rc - (CONV_K - 1)
        acc = jnp.zeros((rc, CONV_WIDTH), F32)
        for j in range(CONV_K):
            acc = acc + hext[base + j:base + j + rc, :] * cw_ref[j:j + 1, :]
        acc = acc + cb_ref[...]
        y = _layernorm(acc, lg_ref[...], lb_ref[...])
        b_ref[c * rc:(c + 1) * rc, :] = (y * jax.nn.sigmoid(y)).astype(BF16)


def _seqmix(proj, bsz, seq, ts, pool_w, pool_scale, conv_w, conv_b, ln_g, ln_b):
    n = bsz * seq
    ns = seq // ts
    hb = ts // HALO

    def cur(width, col):
        return pl.BlockSpec((ts, width), lambda b, i: (b * ns + i, col // width))

    def halo(width, col):
        return pl.BlockSpec((HALO, width), lambda b, i: (jnp.maximum((b * ns + i) * hb - 1, 0), col // width))

    def full(shape):
        return pl.BlockSpec(shape, lambda b, i: (0,) * len(shape))

    return pl.pallas_call(
        functools.partial(_seqmix_kernel, ts=ts),
        grid=(bsz, ns),
        in_specs=[cur(POOL_WIDTH, COL_POOL), halo(POOL_WIDTH, COL_POOL),
                  cur(2 * CONV_WIDTH, COL_CONV), halo(2 * CONV_WIDTH, COL_CONV),
                  full((4, POOL_GROUP, POOL_GROUP)), full((1, POOL_WIDTH)),
                  full((HALO, CONV_WIDTH)), full((1, CONV_WIDTH)), full((1, CONV_WIDTH)), full((1, CONV_WIDTH))],
        out_specs=[pl.BlockSpec((ts, POOL_WIDTH), lambda b, i: (b * ns + i, 0)),
                   pl.BlockSpec((ts, CONV_WIDTH), lambda b, i: (b * ns + i, 0))],
        out_shape=[jax.ShapeDtypeStruct((n, POOL_WIDTH), BF16),
                   jax.ShapeDtypeStruct((n, CONV_WIDTH), BF16)],
        scratch_shapes=[pltpu.VMEM((HALO + ts, POOL_WIDTH), F32),
                        pltpu.VMEM((HALO + ts, CONV_WIDTH), F32)],
        compiler_params=_cparams(("parallel", "arbitrary")),
        name="seqmix",
    )(proj, proj, proj, proj, pool_w, pool_scale, conv_w, conv_b, ln_g, ln_b)


def _rope_tables(positions, rot, period, base):
    lane = np.arange(LANES)
    r = lane % period - base
    active = (r >= 0) & (r < rot)
    f = np.where(active, r % (rot // 2), 0)
    sign = np.where(r < rot // 2, -1.0, 1.0).astype(np.float32)
    inv = jnp.power(ROPE_THETA, -jnp.arange(0, rot, 2, dtype=F32) / rot)
    ang = positions.astype(F32)[..., None] * inv
    cos = jnp.where(active, jnp.cos(ang)[..., f], 1.0)
    sin = jnp.where(active, jnp.sin(ang)[..., f] * sign, 0.0)
    return cos, sin


def _rope128(x, c, s, half, first):
    up = pltpu.roll(x, LANES - half, 1)
    down = pltpu.roll(x, half, 1)
    return x * c + jnp.where(first, up, down) * s


def _rmsnorm(x, g, width):
    ms = jnp.sum(x * x, axis=-1, keepdims=True) * (1.0 / width)
    return x * lax.rsqrt(ms + LN_EPS) * g


def _prep_kernel(cq_ref, ck_ref, cv_ref, mq_ref, mkv_ref, iq_ref, ikw_ref, mkr_ref,
                 cd_ref, sd_ref, ci_ref, si_ref, cm_ref, sm_ref,
                 qn_ref, kvn_ref, wuq_ref, wuk_ref, wuv_ref,
                 qd_ref, kd_ref, vd_ref, qi_ref, ki_ref, wi_ref, qm_ref, km_ref, vm_ref):
    lane = lax.broadcasted_iota(I32, (1, LANES), 1)

    cd, sd = cd_ref[0], sd_ref[0]
    first_d = (lane % DSA_HEAD_DIM) < DSA_ROT // 2
    for c in range(4):
        sl = slice(c * LANES, (c + 1) * LANES)
        qd_ref[0, :, sl] = (_rope128(cq_ref[:, sl], cd, sd, DSA_ROT // 2, first_d)
                            * DSA_HEAD_DIM ** -0.5).astype(BF16)
        kd_ref[0, :, sl] = _rope128(ck_ref[:, sl], cd, sd, DSA_ROT // 2, first_d).astype(BF16)
    vd_ref[0] = cv_ref[...].astype(BF16)

    ci, si = ci_ref[0], si_ref[0]
    first_i = (lane % IDX_DIM) < IDX_ROT // 2
    for c in range(2):
        sl = slice(c * LANES, (c + 1) * LANES)
        qi_ref[0, :, sl] = _rope128(iq_ref[:, sl], ci, si, IDX_ROT // 2, first_i).astype(BF16)
    ikw = ikw_ref[...]
    in_key = lane < IDX_DIM
    kr = _rope128(ikw, jnp.where(in_key, ci, 1.0), jnp.where(in_key, si, 0.0), IDX_ROT // 2, first_i)
    k32 = jnp.where(in_key, kr, 0.0)
    k64 = k32 + pltpu.roll(k32, IDX_DIM, 1)
    ki_ref[0] = (k64 + pltpu.roll(k64, 2 * IDX_DIM, 1)).astype(BF16)
    wi_ref[0] = pltpu.roll(ikw, LANES - IDX_DIM, 1) * (IDX_HEADS ** -0.5 * IDX_DIM ** -0.5)

    cm, sm = cm_ref[0], sm_ref[0]
    first_m = (lane >= MLA_NOPE) & (lane < MLA_NOPE + MLA_ROPE // 2)
    qn = _rmsnorm(mq_ref[...], qn_ref[...], MLA_Q_RANK).astype(BF16)
    q = _dot(qn, wuq_ref[...])
    for h in range(MLA_HEADS):
        sl = slice(h * LANES, (h + 1) * LANES)
        qm_ref[0, :, sl] = _rope128(q[:, sl], cm, sm, MLA_ROPE // 2, first_m).astype(BF16)
    c_kv = _rmsnorm(mkv_ref[...], kvn_ref[...], MLA_KV_RANK).astype(BF16)
    kn = _dot(c_kv, wuk_ref[...])
    vm_ref[0] = _dot(c_kv, wuv_ref[...]).astype(BF16)
    cmk = pltpu.roll(cm, LANES - MLA_NOPE, 1)
    smk = pltpu.roll(sm, LANES - MLA_NOPE, 1)
    krope = _rope128(mkr_ref[...], cmk, smk, MLA_ROPE // 2, lane < MLA_ROPE // 2)
    krope = pltpu.roll(jnp.where(lane < MLA_ROPE, krope, 0.0), MLA_NOPE, 1)
    for h in range(MLA_HEADS):
        sl = slice(h * LANES, (h + 1) * LANES)
        km_ref[0, :, sl] = (kn[:, sl] + krope).astype(BF16)


def _prep(proj, tables, bsz, seq, ts, qn, kvn, wuq, wuk, wuv):
    ns = seq // ts

    def cur(width, col):
        return pl.BlockSpec((ts, width), lambda b, i: (b * ns + i, col // width))

    def tab():
        return pl.BlockSpec((1, ts, LANES), lambda b, i: (b, i, 0))

    def full(shape):
        return pl.BlockSpec(shape, lambda b, i: (0,) * len(shape))

    def out(width):
        return pl.BlockSpec((1, ts, width), lambda b, i: (b, i, 0))

    widths = (512, 512, 512, 256, LANES, LANES, 1024, 1024, 512)
    dtypes = (BF16, BF16, BF16, BF16, BF16, F32, BF16, BF16, BF16)
    return pl.pallas_call(
        _prep_kernel,
        grid=(bsz, ns),
        in_specs=[cur(512, COL_CQ), cur(512, COL_CK), cur(512, COL_CV), cur(512, COL_MQ),
                  cur(256, COL_MKV), cur(256, COL_IQ), cur(LANES, COL_IKW), cur(LANES, COL_MKR)]
                 + [tab() for _ in range(6)]
                 + [full(qn.shape), full(kvn.shape), full(wuq.shape), full(wuk.shape), full(wuv.shape)],
        out_specs=[out(w) for w in widths],
        out_shape=[jax.ShapeDtypeStruct((bsz, seq, w), dt) for w, dt in zip(widths, dtypes)],
        compiler_params=_cparams(("parallel", "arbitrary")),
        name="prep",
    )(*([proj] * 8), *tables, qn, kvn, wuq, wuk, wuv)


def _softmax_step(s, bias, m, l, acc, v, rc):
    rows, width = s.shape
    ps, ms, ls, als = [], [], [], []
    for r0 in range(0, rows, rc):
        cols = []
        for c0 in range(0, width, LANES):
            sc = s[r0:r0 + rc, c0:c0 + LANES]
            if bias is not None:
                rb = r0 % bias.shape[0]
                sc = sc + bias[rb:rb + rc, c0:c0 + LANES]
            cols.append(sc)
        cmax = functools.reduce(jnp.maximum, cols)
        m_old = m[r0:r0 + rc]
        m_new = jnp.maximum(m_old, jnp.max(cmax, axis=1, keepdims=True))
        alpha = jnp.exp(m_old - m_new)
        pcs = [jnp.exp(sc - m_new) for sc in cols]
        psum = functools.reduce(jnp.add, pcs)
        ls.append(alpha * l[r0:r0 + rc] + jnp.sum(psum, axis=1, keepdims=True))
        ms.append(m_new)
        als.append(alpha)
        ps.append(jnp.concatenate(pcs, axis=1).astype(BF16))
    pv = _dot(jnp.concatenate(ps, axis=0), v)
    return jnp.concatenate(ms, axis=0), jnp.concatenate(ls, axis=0), jnp.concatenate(als, axis=0) * acc + pv


def _dsa_kernel(q_ref, k_ref, v_ref, qi_ref, ki_ref, w_ref, o_ref, key_ref, qs_ref, qh_ref, *, tq, tks, n_sel):
    i = pl.program_id(1)
    nkt = ((i + 1) * tq + tks - 1) // tks
    row = i * tq + lax.broadcasted_iota(I32, (tq, 1), 0)
    lane = lax.broadcasted_iota(I32, (1, LANES), 1)

    for h in range(IDX_HEADS):
        qi = qi_ref[0, :, (h // 4) * LANES:(h // 4 + 1) * LANES]
        qs_ref[h * tq:(h + 1) * tq, :] = jnp.where(lane // IDX_DIM == h % 4, qi, jnp.zeros_like(qi))
    for h in range(DSA_HEADS):
        qp = q_ref[0, :, (h // 2) * LANES:(h // 2 + 1) * LANES]
        qh_ref[h // 2, (h % 2) * tq:(h % 2 + 1) * tq, :] = jnp.where(lane // DSA_HEAD_DIM == h % 2, qp,
                                                                     jnp.zeros_like(qp))

    w = w_ref[0]
    w_h = [w[:, h:h + 1] for h in range(IDX_HEADS)]
    hg = 4

    def score_tile(kt, carry):
        k0 = pl.multiple_of(kt * tks, tks)
        kit = ki_ref[0, pl.ds(k0, tks), :]
        acc = jnp.zeros((tq, tks), F32)
        for g in range(IDX_HEADS // hg):
            res = _dot_nt(qs_ref[g * hg * tq:(g + 1) * hg * tq, :], kit)
            for j in range(hg):
                acc = acc + w_h[g * hg + j] * jnp.maximum(res[j * tq:(j + 1) * tq, :], 0.0)
        bits = lax.bitcast_convert_type(acc, I32)
        key = jnp.where(bits >= 0, bits, bits ^ 0x7FFFFFFF)
        col = k0 + lax.broadcasted_iota(I32, (1, tks), 1)
        key_ref[:, pl.ds(k0, tks)] = jnp.where(col <= row, key, INT_MIN)
        return carry

    lax.fori_loop(0, nkt, score_tile, 0)

    def sweep(fn, init):
        def body(kt, part):
            k0 = pl.multiple_of(kt * tks, tks)
            out = []
            for r0 in range(0, tq, 64):
                pr = part[r0:r0 + 64]
                for c in range(tks // LANES):
                    c0 = k0 + c * LANES
                    pr = fn(pr, key_ref[r0:r0 + 64, pl.ds(c0, LANES)], slice(r0, r0 + 64), c0)
                out.append(pr)
            return jnp.concatenate(out, axis=0)
        return lax.fori_loop(0, nkt, body, init)

    def count_ge(cand):
        part = sweep(lambda pr, kk, rs, c0: pr + jnp.where(kk >= cand[rs], 1.0, 0.0), jnp.zeros((tq, LANES), F32))
        return jnp.sum(part, axis=1, keepdims=True)

    zero = jnp.zeros((tq, LANES), I32)
    c_zero = count_ge(zero)
    nonneg = c_zero >= n_sel
    base = jnp.where(nonneg, zero, INT_MIN)
    c_base = jnp.where(nonneg, c_zero, (row + 1).astype(F32))

    def bit_body(j, st):
        base, c_base = st
        cand = base | jnp.left_shift(jnp.int32(1), 30 - j)
        c = count_ge(cand)
        ok = c >= n_sel
        return jnp.where(ok, cand, base), jnp.where(ok, c, c_base)

    thr, c_thr = lax.fori_loop(0, 31, bit_body, (base, c_base))
    thr = jnp.maximum(thr, INT_MIN + 1)

    def fix_ties(rs):
        thr_c = thr[rs]

        def count_if(pred):
            def body(kt, part):
                k0 = pl.multiple_of(kt * tks, tks)
                for c in range(tks // LANES):
                    c0 = k0 + c * LANES
                    part = part + jnp.where(pred(key_ref[rs, pl.ds(c0, LANES)], c0), 1.0, 0.0)
                return part
            part = lax.fori_loop(0, nkt, body, jnp.zeros((64, LANES), F32))
            return jnp.sum(part, axis=1, keepdims=True)

        need = n_sel - count_if(lambda kk, c0: kk > thr_c)

        def col_body(_, st):
            j_lo, j_hi = st
            mid = (j_lo + j_hi) >> 1
            ok = count_if(lambda kk, c0: jnp.logical_and(kk == thr_c, c0 + lane <= mid)) >= need
            return jnp.where(ok, j_lo, mid), jnp.where(ok, mid, j_hi)

        j0 = (jnp.full((64, LANES), -1, I32), jnp.full((64, LANES), 0, I32) + (nkt * tks - 1))
        _, last = lax.fori_loop(0, key_ref.shape[1].bit_length(), col_body, j0)

        def demote(kt, carry):
            k0 = pl.multiple_of(kt * tks, tks)
            for c in range(tks // LANES):
                c0 = k0 + c * LANES
                kk = key_ref[rs, pl.ds(c0, LANES)]
                drop = jnp.logical_and(kk == thr_c, c0 + lane > last)
                key_ref[rs, pl.ds(c0, LANES)] = jnp.where(drop, thr_c - 1, kk)
            return carry

        lax.fori_loop(0, nkt, demote, 0)

    for r0 in range(0, tq, 64):
        rs = slice(r0, r0 + 64)
        pl.when(jnp.max(jnp.where(c_thr[rs] > n_sel, 1, 0)) > 0)(functools.partial(fix_ties, rs))

    def att_tile(kt, carry):
        k0 = pl.multiple_of(kt * tks, tks)
        bias = jnp.concatenate([jnp.where(key_ref[:, pl.ds(k0 + c * LANES, LANES)] >= thr, 0.0, NEG_BIG)
                                for c in range(tks // LANES)], axis=1)
        out = []
        for p in range(DSA_HEADS // 2):
            cols = slice(p * LANES, (p + 1) * LANES)
            m, l, acc = carry[p]
            s = _dot_nt(qh_ref[p], k_ref[0, pl.ds(k0, tks), cols])
            out.append(_softmax_step(s, bias, m, l, acc, v_ref[0, pl.ds(k0, tks), cols], 64))
        return tuple(out)

    init = tuple((jnp.full((2 * tq, LANES), NEG_BIG, F32), jnp.zeros((2 * tq, LANES), F32),
                  jnp.zeros((2 * tq, LANES), F32)) for _ in range(DSA_HEADS // 2))
    fin = lax.fori_loop(0, nkt, att_tile, init)
    for p in range(DSA_HEADS // 2):
        o = fin[p][2] / fin[p][1]
        o_ref[0, :, p * LANES:(p + 1) * LANES] = jnp.where(lane < DSA_HEAD_DIM, o[:tq], o[tq:]).astype(BF16)


def _dsa(qd, kd, vd, qi, ki, wi, tq, tks):
    bsz, seq, _ = qd.shape
    n_sel = min(DSA_TOPK, seq // 4)
    one = pl.Buffered(1)

    def blk(width):
        return pl.BlockSpec((1, tq, width), lambda b, i: (b, i, 0))

    def res(width):
        return pl.BlockSpec((1, seq, width), lambda b, i: (b, 0, 0), pipeline_mode=one)

    return pl.pallas_call(
        functools.partial(_dsa_kernel, tq=tq, tks=tks, n_sel=n_sel),
        grid=(bsz, seq // tq),
        in_specs=[blk(512), res(512), res(512), blk(256), res(LANES), blk(LANES)],
        out_specs=blk(512),
        out_shape=jax.ShapeDtypeStruct((bsz, seq, 512), BF16),
        scratch_shapes=[pltpu.VMEM((tq, seq), I32),
                        pltpu.VMEM((IDX_HEADS * tq, LANES), BF16),
                        pltpu.VMEM((DSA_HEADS // 2, 2 * tq, LANES), BF16)],
        compiler_params=_cparams(("parallel", "arbitrary")),
        name="dsa",
    )(qd, kd, vd, qi, ki, wi)


def _mla_kernel(q_ref, k_ref, v_ref, o_ref, *, tq):
    i = pl.program_id(2)
    scale = (MLA_NOPE + MLA_ROPE) ** -0.5
    lane = lax.broadcasted_iota(I32, (1, LANES), 1)

    def step(k0, carry, diag):
        vt = v_ref[0, pl.ds(k0, tq), :]
        out = []
        for hh in range(2):
            cols = slice(hh * LANES, (hh + 1) * LANES)
            m, l, acc = carry[hh]
            s = _dot_nt(q_ref[0, :, cols], k_ref[0, pl.ds(k0, tq), cols]) * scale
            bias = None
            if diag:
                rowi = lax.broadcasted_iota(I32, (tq, tq), 0)
                coli = lax.broadcasted_iota(I32, (tq, tq), 1)
                bias = jnp.where(coli <= rowi, 0.0, NEG_BIG)
            out.append(_softmax_step(s, bias, m, l, acc, vt, 64))
        return tuple(out)

    init = tuple((jnp.full((tq, LANES), NEG_BIG, F32), jnp.zeros((tq, LANES), F32), jnp.zeros((tq, LANES), F32))
                 for _ in range(2))
    carry = lax.fori_loop(0, i, lambda kt, c: step(pl.multiple_of(kt * tq, tq), c, False), init)
    fin = step(pl.multiple_of(i * tq, tq), carry, True)
    o_ref[0] = jnp.where(lane < MLA_V, fin[0][2] / fin[0][1], fin[1][2] / fin[1][1]).astype(BF16)


def _mla(qm, km, vm, tq):
    bsz, seq, _ = qm.shape
    return pl.pallas_call(
        functools.partial(_mla_kernel, tq=tq),
        grid=(bsz, MLA_HEADS // 2, seq // tq),
        in_specs=[pl.BlockSpec((1, tq, 2 * LANES), lambda b, p, i: (b, i, p)),
                  pl.BlockSpec((1, seq, 2 * LANES), lambda b, p, i: (b, 0, p)),
                  pl.BlockSpec((1, seq, LANES), lambda b, p, i: (b, 0, p))],
        out_specs=pl.BlockSpec((1, tq, LANES), lambda b, p, i: (b, i, p)),
        out_shape=jax.ShapeDtypeStruct((bsz, seq, MLA_HEADS * MLA_V), BF16),
        compiler_params=_cparams(("parallel", "parallel", "arbitrary")),
        name="mla",
    )(qm, km, vm)


def _merge_kernel(a_ref, b_ref, c_ref, d_ref, g_ref, x_ref, wa_ref, wb_ref, wc_ref, wd_ref, wo_ref,
                  lg_ref, lb_ref, h_ref):
    merged = None
    for j, (br, w) in enumerate(((a_ref, wa_ref), (b_ref, wb_ref), (c_ref, wc_ref), (d_ref, wd_ref))):
        y = jax.nn.sigmoid(g_ref[:, j * D_MODEL:(j + 1) * D_MODEL]) * _dot(br[...], w[...])
        merged = y if merged is None else merged + y
    out = _dot(merged.astype(BF16), wo_ref[...])
    h_ref[...] = _layernorm(DEEPNORM_ALPHA * x_ref[...] + out, lg_ref[...], lb_ref[...])


def _merge(ya, yb, yc, yd, proj, x, wa, wb, wc, wd, wo, lg, lb, tm):
    n = x.shape[0]

    def rows(width):
        return pl.BlockSpec((tm, width), lambda i: (i, 0))

    def full(shape):
        return pl.BlockSpec(shape, lambda i: (0,) * len(shape))

    return pl.pallas_call(
        _merge_kernel,
        grid=(n // tm,),
        in_specs=[rows(512), rows(512), rows(512), rows(512), rows(N_BRANCH * D_MODEL), rows(D_MODEL),
                  full(wa.shape), full(wb.shape), full(wc.shape), full(wd.shape), full(wo.shape),
                  full(lg.shape), full(lb.shape)],
        out_specs=rows(D_MODEL),
        out_shape=jax.ShapeDtypeStruct((n, D_MODEL), F32),
        compiler_params=_cparams(("parallel",)),
        name="merge",
    )(ya, yb, yc, yd, proj, x, wa, wb, wc, wd, wo, lg, lb)


def _router_kernel(h_ref, wh_ref, wl_ref, b_ref, eidx_ref, gate_ref, rank_ref, cnt_ref, carry_ref, *, tm):
    i = pl.program_id(0)

    @pl.when(i == 0)
    def _():
        carry_ref[...] = jnp.zeros_like(carry_ref)

    h = h_ref[...]
    hh = h.astype(BF16)
    hl = (h - hh.astype(F32)).astype(BF16)
    wh = wh_ref[...]
    logits = _dot(hh, wh) + _dot(hl, wh) + _dot(hh, wl_ref[...]) + b_ref[...]
    lane = lax.broadcasted_iota(I32, (1, LANES), 1)
    lane_f = lane.astype(F32)
    vals, idxs = [], []
    cur = logits
    for _ in range(TOP_K):
        m = jnp.max(cur, axis=1, keepdims=True)
        ix = jnp.min(jnp.where(cur == m, lane_f, float(LANES)), axis=1, keepdims=True)
        vals.append(m)
        idxs.append(ix)
        cur = jnp.where(lane_f == ix, -jnp.inf, cur)
    ex = [jnp.exp(v - vals[0]) for v in vals]
    den = ex[0] + ex[1] + ex[2] + ex[3]
    hot = [lane_f == ix for ix in idxs]
    cnt_tok = sum(jnp.where(hk, 1.0, 0.0) for hk in hot)
    r_i = lax.broadcasted_iota(I32, (tm, tm), 0)
    c_i = lax.broadcasted_iota(I32, (tm, tm), 1)
    lower = jnp.where(c_i < r_i, 1.0, 0.0).astype(BF16)
    prefix = _dot(lower, cnt_tok.astype(BF16)) + carry_ref[...]
    eidx = jnp.zeros((tm, LANES), F32)
    gate = jnp.zeros((tm, LANES), F32)
    rank = jnp.zeros((tm, LANES), F32)
    for k in range(TOP_K):
        rk = jnp.sum(jnp.where(hot[k], prefix, 0.0), axis=1, keepdims=True)
        eidx = jnp.where(lane == k, idxs[k], eidx)
        gate = jnp.where(lane == k, ex[k] / den, gate)
        rank = jnp.where(lane == k, rk, rank)
    eidx_ref[...] = eidx.astype(I32)
    gate_ref[...] = gate
    rank_ref[...] = rank.astype(I32)
    carry_ref[...] = carry_ref[...] + jnp.sum(cnt_tok, axis=0, keepdims=True)
    cnt_ref[...] = carry_ref[...].astype(I32)


def _router(h, wh, wl, b, tm):
    n = h.shape[0]

    def rows(width):
        return pl.BlockSpec((tm, width), lambda i: (i, 0))

    def full(shape):
        return pl.BlockSpec(shape, lambda i: (0,) * len(shape))

    return pl.pallas_call(
        functools.partial(_router_kernel, tm=tm),
        grid=(n // tm,),
        in_specs=[rows(D_MODEL), full(wh.shape), full(wl.shape), full(b.shape)],
        out_specs=[rows(LANES), rows(LANES), rows(LANES), full((1, LANES))],
        out_shape=[jax.ShapeDtypeStruct((n, LANES), I32), jax.ShapeDtypeStruct((n, LANES), F32),
                   jax.ShapeDtypeStruct((n, LANES), I32), jax.ShapeDtypeStruct((1, LANES), I32)],
        scratch_shapes=[pltpu.VMEM((1, LANES), F32)],
        compiler_params=_cparams(("arbitrary",)),
        name="router",
    )(h, wh, wl, b)


def _row_copy(src_hbm, row, dst, r, sem):
    return pltpu.make_async_copy(src_hbm.at[pl.ds(row, 1), :], dst.at[pl.ds(r, 1), :], sem)


def _idx_copy(idx_hbm, blk, idx_smem, isem):
    return pltpu.make_async_copy(idx_hbm.at[blk, 0], idx_smem, isem)


def _gather_rows(idx_smem, src_hbm, dst, sem, nrows, inline):
    if inline:
        for r in range(nrows):
            _row_copy(src_hbm, idx_smem[r], dst, r, sem).start()
        return

    def body(r, carry):
        _row_copy(src_hbm, idx_smem[r], dst, r, sem).start()
        return carry

    lax.fori_loop(0, nrows, body, 0, unroll=8)


def _wait_rows(src_hbm, dst, sem, nrows):
    pltpu.make_async_copy(src_hbm.at[pl.ds(0, nrows), :], dst, sem).wait()


def _expert_kernel(be_ref, idx_hbm, h_hbm, wgu_ref, bgu_ref, wd_ref, bd_ref, y_ref,
                   xbuf, idx_smem, sem, isem, *, rb, nb):
    del be_ref
    i = pl.program_id(0)
    slot = i % 2

    @pl.when(i == 0)
    def _():
        first = _idx_copy(idx_hbm, 0, idx_smem, isem)
        first.start()
        first.wait()
        _gather_rows(idx_smem, h_hbm, xbuf.at[0], sem.at[0], rb, False)
        _idx_copy(idx_hbm, min(1, nb - 1), idx_smem, isem).start()

    _wait_rows(h_hbm, xbuf.at[slot], sem.at[slot], rb)
    _idx_copy(idx_hbm, 0, idx_smem, isem).wait()
    _gather_rows(idx_smem, h_hbm, xbuf.at[1 - slot], sem.at[1 - slot], rb, True)
    xb = xbuf[slot].astype(BF16)
    fc = 512
    y = None
    for c in range(D_FF // fc):
        g = _dot(xb, wgu_ref[0, :, c * fc:(c + 1) * fc]) + bgu_ref[0, :, c * fc:(c + 1) * fc]
        lin = (_dot(xb, wgu_ref[0, :, D_FF + c * fc:D_FF + (c + 1) * fc])
               + bgu_ref[0, :, D_FF + c * fc:D_FF + (c + 1) * fc])
        g = jnp.minimum(g, SWIGLU_LIMIT)
        lin = jnp.clip(lin, -SWIGLU_LIMIT, SWIGLU_LIMIT)
        act = (g * jax.nn.sigmoid(SWIGLU_ALPHA * g) * (lin + 1.0)).astype(BF16)
        part = _dot(act, wd_ref[0, c * fc:(c + 1) * fc, :])
        y = part if y is None else y + part
    y_ref[...] = y + bd_ref[0]

    @pl.when(i < nb - 1)
    def _():
        _idx_copy(idx_hbm, jnp.minimum(i + 2, nb - 1), idx_smem, isem).start()

    @pl.when(i == nb - 1)
    def _():
        _wait_rows(h_hbm, xbuf.at[1 - slot], sem.at[1 - slot], rb)


def _experts(blk_expert, row_src, h, wgu, bgu, wd, bd, rb):
    nb = blk_expert.shape[0]
    grid_spec = pltpu.PrefetchScalarGridSpec(
        num_scalar_prefetch=1,
        grid=(nb,),
        in_specs=[pl.BlockSpec(memory_space=pl.ANY),
                  pl.BlockSpec(memory_space=pl.ANY),
                  pl.BlockSpec((1, D_MODEL, 2 * D_FF), lambda i, be: (be[i], 0, 0)),
                  pl.BlockSpec((1, 1, 2 * D_FF), lambda i, be: (be[i], 0, 0)),
                  pl.BlockSpec((1, D_FF, D_MODEL), lambda i, be: (be[i], 0, 0)),
                  pl.BlockSpec((1, 1, D_MODEL), lambda i, be: (be[i], 0, 0))],
        out_specs=pl.BlockSpec((rb, D_MODEL), lambda i, be: (i, 0)),
        scratch_shapes=[pltpu.VMEM((2, rb, D_MODEL), F32),
                        pltpu.SMEM((rb,), I32),
                        pltpu.SemaphoreType.DMA((2,)),
                        pltpu.SemaphoreType.DMA(())],
    )
    return pl.pallas_call(
        functools.partial(_expert_kernel, rb=rb, nb=nb),
        grid_spec=grid_spec,
        out_shape=jax.ShapeDtypeStruct((nb * rb, D_MODEL), F32),
        compiler_params=_cparams(("arbitrary",)),
        name="experts",
    )(blk_expert, row_src, h, wgu, bgu, wd, bd)


def _combine_kernel(idx_hbm, y_hbm, gate_ref, h_ref, lg_ref, lb_ref, x_ref, xb_ref,
                    ybuf, idx_smem, sem, isem, *, tm, nt):
    i = pl.program_id(0)
    slot = i % 2

    @pl.when(i == 0)
    def _():
        first = _idx_copy(idx_hbm, 0, idx_smem, isem)
        first.start()
        first.wait()
        _gather_rows(idx_smem, y_hbm, ybuf.at[0], sem.at[0], TOP_K * tm, False)
        _idx_copy(idx_hbm, min(1, nt - 1), idx_smem, isem).start()

    _wait_rows(y_hbm, ybuf.at[slot], sem.at[slot], TOP_K * tm)
    _idx_copy(idx_hbm, 0, idx_smem, isem).wait()
    _gather_rows(idx_smem, y_hbm, ybuf.at[1 - slot], sem.at[1 - slot], TOP_K * tm, True)
    gate = gate_ref[...]
    moe = None
    for k in range(TOP_K):
        t = ybuf[slot, k * tm:(k + 1) * tm, :] * gate[:, k:k + 1]
        moe = t if moe is None else moe + t
    out = _layernorm(DEEPNORM_ALPHA * h_ref[...] + moe, lg_ref[...], lb_ref[...])
    x_ref[...] = out
    xb_ref[...] = out.astype(BF16)

    @pl.when(i < nt - 1)
    def _():
        _idx_copy(idx_hbm, jnp.minimum(i + 2, nt - 1), idx_smem, isem).start()

    @pl.when(i == nt - 1)
    def _():
        _wait_rows(y_hbm, ybuf.at[1 - slot], sem.at[1 - slot], TOP_K * tm)


def _combine(dest_t, ybuf, gate, h, lg, lb, tm):
    n = h.shape[0]
    nt = n // tm

    def rows(width):
        return pl.BlockSpec((tm, width), lambda i: (i, 0))

    def full(shape):
        return pl.BlockSpec(shape, lambda i: (0,) * len(shape))

    return pl.pallas_call(
        functools.partial(_combine_kernel, tm=tm, nt=nt),
        grid=(nt,),
        in_specs=[pl.BlockSpec(memory_space=pl.ANY),
                  pl.BlockSpec(memory_space=pl.ANY),
                  rows(LANES), rows(D_MODEL), full(lg.shape), full(lb.shape)],
        out_specs=[rows(D_MODEL), rows(D_MODEL)],
        out_shape=[jax.ShapeDtypeStruct((n, D_MODEL), F32), jax.ShapeDtypeStruct((n, D_MODEL), BF16)],
        scratch_shapes=[pltpu.VMEM((2, TOP_K * tm, D_MODEL), F32),
                        pltpu.SMEM((TOP_K * tm,), I32),
                        pltpu.SemaphoreType.DMA((2,)),
                        pltpu.SemaphoreType.DMA(())],
        compiler_params=_cparams(("arbitrary",)),
        name="combine",
    )(dest_t, ybuf, gate, h, lg, lb)


def _pad_cols(w, width):
    return jnp.pad(w, ((0, 0), (0, width - w.shape[1])))


def _pack_w_in(w_in):
    offs = np.cumsum((0,) + SPLITS)
    seg = [w_in[:, int(offs[j]):int(offs[j + 1])] for j in range(len(SPLITS))]
    u_pool, u_conv, c_q, c_k, c_v, i_q, i_k, i_w, m_q, m_kv, m_kr, g_all = seg
    parts = [g_all, u_conv, u_pool, c_q, c_k, c_v, _pad_cols(m_q, 512), m_kv, i_q,
             _pad_cols(jnp.concatenate([i_k, i_w], axis=1), LANES), _pad_cols(m_kr, LANES)]
    return jnp.concatenate(parts, axis=1).astype(BF16)


def _pack_heads(w, widths_in, total):
    r = w.shape[0]
    per = sum(widths_in)
    w = w.reshape(r, -1, per)
    w = jnp.pad(w, ((0, 0), (0, 0), (0, total - per)))
    return w.reshape(r, -1)


def _layer(x, xb, tables, bsz, seq, cfg, w_in, pool_w, pool_scale, pool_out, conv_w, conv_b, conv_ln_g,
           conv_ln_b, conv_out, dsa_out, mla_q_norm, mla_kv_norm, mla_wuq, mla_wuk, mla_wuv, mla_out, w_o,
           ln1_g, ln1_b, router_w, router_b, exp_w_gu, exp_b_gu, exp_w_d, exp_b_d, ln2_g, ln2_b):
    n = bsz * seq
    row = lambda v: v.reshape(1, -1).astype(F32)

    proj = _inproj(xb, _pack_w_in(w_in), cfg["tm_in"], cfg["tn_in"])

    ya, yb = _seqmix(proj, bsz, seq, cfg["ts"], pool_w.astype(BF16), row(pool_scale),
                     jnp.pad(conv_w, ((0, HALO - CONV_K), (0, 0))), row(conv_b), row(conv_ln_g), row(conv_ln_b))

    wuq = jnp.pad(_pack_heads(mla_wuq, (MLA_NOPE, MLA_ROPE), LANES), ((0, 512 - MLA_Q_RANK), (0, 0))).astype(BF16)
    wuk = _pack_heads(mla_wuk, (MLA_NOPE,), LANES).astype(BF16)
    qd, kd, vd, qi, ki, wi, qm, km, vm = _prep(
        proj, tables, bsz, seq, cfg["ts"], _pad_cols(row(mla_q_norm), 512), row(mla_kv_norm),
        wuq, wuk, mla_wuv.astype(BF16))

    yc = _dsa(qd, kd, vd, qi, ki, wi, cfg["tq_dsa"], cfg["tk_dsa"]).reshape(n, -1)
    yd = _mla(qm, km, vm, cfg["tq_mla"]).reshape(n, -1)

    h = _merge(ya, yb, yc, yd, proj, x, pool_out.astype(BF16), conv_out.astype(BF16), dsa_out.astype(BF16),
               mla_out.astype(BF16), w_o.astype(BF16), row(ln1_g), row(ln1_b), cfg["tm_merge"])

    rw = _pad_cols(router_w, LANES)
    rwh = rw.astype(BF16)
    rwl = (rw - rwh.astype(F32)).astype(BF16)
    rb_ = jnp.pad(row(router_b), ((0, 0), (0, LANES - N_EXPERTS)), constant_values=NEG_BIG)
    eidx, gate, rank, counts = _router(h, rwh, rwl, rb_, cfg["tm_router"])

    counts = counts[0, :N_EXPERTS]
    padded = (counts + MOE_BLOCK - 1) // MOE_BLOCK * MOE_BLOCK
    pad_end = jnp.cumsum(padded)
    pad_start = pad_end - padded
    n_blocks = -(-(n * TOP_K) // MOE_BLOCK) + N_EXPERTS
    dest = pad_start[eidx[:, :TOP_K]] + rank[:, :TOP_K]
    tok = jnp.broadcast_to(jnp.arange(n, dtype=I32)[:, None], (n, TOP_K))
    row_src = jnp.zeros((n_blocks * MOE_BLOCK,), I32).at[dest.reshape(-1)].set(tok.reshape(-1), unique_indices=True)
    blk_start = jnp.arange(n_blocks, dtype=I32) * MOE_BLOCK
    blk_expert = jnp.minimum(jnp.sum((pad_end[None, :] <= blk_start[:, None]).astype(I32), axis=1), N_EXPERTS - 1)

    ybuf = _experts(blk_expert, row_src.reshape(n_blocks, 1, MOE_BLOCK), h,
                    exp_w_gu.astype(BF16), exp_b_gu.reshape(N_EXPERTS, 1, -1),
                    exp_w_d.astype(BF16), exp_b_d.reshape(N_EXPERTS, 1, -1), MOE_BLOCK)

    tmc = cfg["tm_comb"]
    dest_t = dest.reshape(n // tmc, tmc, TOP_K).transpose(0, 2, 1).reshape(n // tmc, 1, TOP_K * tmc)
    return _combine(dest_t, ybuf, gate, h, row(ln2_g), row(ln2_b), tmc)


_CFG = dict(tm_in=1024, tn_in=768, ts=512, tq_dsa=256, tk_dsa=512, tq_mla=512, tm_merge=256,
            tm_router=512, tm_comb=128)


def _forward(cfg, x, positions, *weights):
    bsz, seq, d = x.shape
    n = bsz * seq
    tables = (*_rope_tables(positions, DSA_ROT, DSA_HEAD_DIM, 0),
              *_rope_tables(positions, IDX_ROT, IDX_DIM, 0),
              *_rope_tables(positions, MLA_ROPE, LANES, MLA_NOPE))
    xf = x.reshape(n, d)
    xb = xf.astype(BF16)
    for l in range(DEPTH):
        xf, xb = _layer(xf, xb, tables, bsz, seq, cfg, *[w[l] for w in weights])
    return xf.reshape(bsz, seq, d)


def kernel(x, positions, w_in, pool_w, pool_scale, pool_out, conv_w, conv_b, conv_ln_g, conv_ln_b, conv_out,
           dsa_out, mla_q_norm, mla_kv_norm, mla_wuq, mla_wuk, mla_wuv, mla_out, w_o, ln1_g, ln1_b, router_w,
           router_b, exp_w_gu, exp_b_gu, exp_w_d, exp_b_d, ln2_g, ln2_b):
    return _forward(_CFG, x, positions, w_in, pool_w, pool_scale, pool_out, conv_w, conv_b, conv_ln_g,
                    conv_ln_b, conv_out, dsa_out, mla_q_norm, mla_kv_norm, mla_wuq, mla_wuk, mla_wuv, mla_out,
                    w_o, ln1_g, ln1_b, router_w, router_b, exp_w_gu, exp_b_gu, exp_w_d, exp_b_d, ln2_g, ln2_b)
```

```python
import functools

import numpy as np
import jax
import jax.numpy as jnp
from jax import lax
from jax.experimental import pallas as pl
from jax.experimental.pallas import tpu as pltpu

F32 = jnp.float32
BF16 = jnp.bfloat16
I32 = jnp.int32

D_MODEL = 1024
DEPTH = 2
ROPE_THETA = 500000.0
LN_EPS = 1e-5
POOL_WIDTH = 512
POOL_WINDOWS = (2, 4, 8, 16)
POOL_GROUP = POOL_WIDTH // 4
CONV_WIDTH = 512
CONV_K = 31
DSA_HEADS = 8
DSA_HEAD_DIM = 64
DSA_ROT = DSA_HEAD_DIM // 4
IDX_HEADS = 8
IDX_DIM = 32
IDX_ROT = IDX_DIM // 4
DSA_TOPK = 256
MLA_HEADS = 8
MLA_NOPE = 64
MLA_ROPE = 32
MLA_V = 64
MLA_Q_RANK = 384
MLA_KV_RANK = 256
N_BRANCH = 4
N_EXPERTS = 32
TOP_K = 4
D_FF = 1024
SWIGLU_ALPHA = 1.702
SWIGLU_LIMIT = 7.0
MOE_BLOCK = 512
DEEPNORM_ALPHA = (2 * DEPTH) ** 0.25
SPLITS = (POOL_WIDTH, 2 * CONV_WIDTH,
          DSA_HEADS * DSA_HEAD_DIM, DSA_HEADS * DSA_HEAD_DIM, DSA_HEADS * DSA_HEAD_DIM,
          IDX_HEADS * IDX_DIM, IDX_DIM, IDX_HEADS,
          MLA_Q_RANK, MLA_KV_RANK, MLA_ROPE,
          N_BRANCH * D_MODEL)

LANES = 128
HALO = 32
VMEM_LIMIT = 56 * 1024 * 1024

COL_GATE = 0
COL_CONV = 4096
COL_POOL = 5120
COL_CQ = 5632
COL_CK = 6144
COL_CV = 6656
COL_MQ = 7168
COL_MKV = 7680
COL_IQ = 7936
COL_IKW = 8192
COL_MKR = 8320
P_COLS = 8448

INT_MIN = -(2 ** 31)
NEG_BIG = -1e30


def _cparams(sem, vmem=VMEM_LIMIT):
    return pltpu.CompilerParams(dimension_semantics=sem, vmem_limit_bytes=vmem)


def _dot(a, b):
    return jnp.dot(a, b, preferred_element_type=F32)


def _dot_nt(a, b):
    return lax.dot_general(a, b, (((1,), (1,)), ((), ())), preferred_element_type=F32)


def _layernorm(z, g, b):
    mu = jnp.mean(z, axis=-1, keepdims=True)
    zc = z - mu
    var = jnp.mean(zc * zc, axis=-1, keepdims=True)
    return zc * lax.rsqrt(var + LN_EPS) * g + b


def _matmul_kernel(x_ref, w_ref, o_ref):
    o_ref[...] = _dot(x_ref[...], w_ref[...])


def _inproj(xb, wp, tm, tn):
    n, d = xb.shape
    p = wp.shape[1]
    return pl.pallas_call(
        _matmul_kernel,
        grid=(n // tm, p // tn),
        in_specs=[pl.BlockSpec((tm, d), lambda i, j: (i, 0)),
                  pl.BlockSpec((d, tn), lambda i, j: (0, j))],
        out_specs=pl.BlockSpec((tm, tn), lambda i, j: (i, j)),
        out_shape=jax.ShapeDtypeStruct((n, p), F32),
        compiler_params=_cparams(("parallel", "arbitrary")),
        name="inproj",
    )(xb, wp)


def _seqmix_kernel(up_ref, uph_ref, uc_ref, uch_ref, pw_ref, ps_ref, cw_ref, cb_ref, lg_ref, lb_ref,
                   a_ref, b_ref, pext, hext, *, ts):
    i = pl.program_id(1)
    has_prev = i > 0

    cur = up_ref[...]
    pext[0:HALO, :] = jnp.where(has_prev, uph_ref[...], 0.0)
    pext[HALO:HALO + ts, :] = cur
    pos = i * ts + lax.broadcasted_iota(I32, (ts, 1), 0)
    for gi, w in enumerate(POOL_WINDOWS):
        cols = slice(gi * POOL_GROUP, (gi + 1) * POOL_GROUP)
        acc = cur[:, cols]
        for d in range(1, w):
            acc = acc + pext[HALO - d:HALO - d + ts, cols]
        cnt = jnp.minimum(pos + 1, w).astype(F32)
        pooled = acc / cnt - cur[:, cols]
        mixed = _dot(pooled.astype(BF16), pw_ref[gi])
        a_ref[:, cols] = (mixed * ps_ref[:, cols]).astype(BF16)

    def glu(u):
        return u[:, :CONV_WIDTH] * jax.nn.sigmoid(u[:, CONV_WIDTH:])

    hext[0:HALO, :] = jnp.where(has_prev, glu(uch_ref[...]), 0.0)
    hext[HALO:HALO + ts, :] = glu(uc_ref[...])
    rc = 64
    for c in range(ts // rc):
        base = HALO + c * rc - (CONV_K - 1)
        acc = jnp.zeros((rc, CONV_WIDTH), F32)
        for j in range(CONV_K):
            acc = acc + hext[base + j:base + j + rc, :] * cw_ref[j:j + 1, :]
        acc = acc + cb_ref[...]
        y = _layernorm(acc, lg_ref[...], lb_ref[...])
        b_ref[c * rc:(c + 1) * rc, :] = (y * jax.nn.sigmoid(y)).astype(BF16)


def _seqmix(proj, bsz, seq, ts, pool_w, pool_scale, conv_w, conv_b, ln_g, ln_b):
    n = bsz * seq
    ns = seq // ts
    hb = ts // HALO

    def cur(width, col):
        return pl.BlockSpec((ts, width), lambda b, i: (b * ns + i, col // width))

    def halo(width, col):
        return pl.BlockSpec((HALO, width), lambda b, i: (jnp.maximum((b * ns + i) * hb - 1, 0), col // width))

    def full(shape):
        return pl.BlockSpec(shape, lambda b, i: (0,) * len(shape))

    return pl.pallas_call(
        functools.partial(_seqmix_kernel, ts=ts),
        grid=(bsz, ns),
        in_specs=[cur(POOL_WIDTH, COL_POOL), halo(POOL_WIDTH, COL_POOL),
                  cur(2 * CONV_WIDTH, COL_CONV), halo(2 * CONV_WIDTH, COL_CONV),
                  full((4, POOL_GROUP, POOL_GROUP)), full((1, POOL_WIDTH)),
                  full((HALO, CONV_WIDTH)), full((1, CONV_WIDTH)), full((1, CONV_WIDTH)), full((1, CONV_WIDTH))],
        out_specs=[pl.BlockSpec((ts, POOL_WIDTH), lambda b, i: (b * ns + i, 0)),
                   pl.BlockSpec((ts, CONV_WIDTH), lambda b, i: (b * ns + i, 0))],
        out_shape=[jax.ShapeDtypeStruct((n, POOL_WIDTH), BF16),
                   jax.ShapeDtypeStruct((n, CONV_WIDTH), BF16)],
        scratch_shapes=[pltpu.VMEM((HALO + ts, POOL_WIDTH), F32),
                        pltpu.VMEM((HALO + ts, CONV_WIDTH), F32)],
        compiler_params=_cparams(("parallel", "arbitrary")),
        name="seqmix",
    )(proj, proj, proj, proj, pool_w, pool_scale, conv_w, conv_b, ln_g, ln_b)


def _rope_tables(positions, rot, period, base):
    lane = np.arange(LANES)
    r = lane % period - base
    active = (r >= 0) & (r < rot)
    f = np.where(active, r % (rot // 2), 0)
    sign = np.where(r < rot // 2, -1.0, 1.0).astype(np.float32)
    inv = jnp.power(ROPE_THETA, -jnp.arange(0, rot, 2, dtype=F32) / rot)
    ang = positions.astype(F32)[..., None] * inv
    cos = jnp.where(active, jnp.cos(ang)[..., f], 1.0)
    sin = jnp.where(active, jnp.sin(ang)[..., f] * sign, 0.0)
    return cos, sin


def _rope128(x, c, s, half, first):
    up = pltpu.roll(x, LANES - half, 1)
    down = pltpu.roll(x, half, 1)
    return x * c + jnp.where(first, up, down) * s


def _rmsnorm(x, g, width):
    ms = jnp.sum(x * x, axis=-1, keepdims=True) * (1.0 / width)
    return x * lax.rsqrt(ms + LN_EPS) * g


def _prep_kernel(cq_ref, ck_ref, cv_ref, mq_ref, mkv_ref, iq_ref, ikw_ref, mkr_ref,
                 cd_ref, sd_ref, ci_ref, si_ref, cm_ref, sm_ref,
                 qn_ref, kvn_ref, wuq_ref, wuk_ref, wuv_ref,
                 qd_ref, kd_ref, vd_ref, qi_ref, ki_ref, wi_ref, qm_ref, km_ref, vm_ref):
    lane = lax.broadcasted_iota(I32, (1, LANES), 1)

    cd, sd = cd_ref[0], sd_ref[0]
    first_d = (lane % DSA_HEAD_DIM) < DSA_ROT // 2
    for c in range(4):
        sl = slice(c * LANES, (c + 1) * LANES)
        qd_ref[0, :, sl] = (_rope128(cq_ref[:, sl], cd, sd, DSA_ROT // 2, first_d)
                            * DSA_HEAD_DIM ** -0.5).astype(BF16)
        kd_ref[0, :, sl] = _rope128(ck_ref[:, sl], cd, sd, DSA_ROT // 2, first_d).astype(BF16)
    vd_ref[0] = cv_ref[...].astype(BF16)

    ci, si = ci_ref[0], si_ref[0]
    first_i = (lane % IDX_DIM) < IDX_ROT // 2
    for c in range(2):
        sl = slice(c * LANES, (c + 1) * LANES)
        qi_ref[0, :, sl] = _rope128(iq_ref[:, sl], ci, si, IDX_ROT // 2, first_i).astype(BF16)
    ikw = ikw_ref[...]
    in_key = lane < IDX_DIM
    kr = _rope128(ikw, jnp.where(in_key, ci, 1.0), jnp.where(in_key, si, 0.0), IDX_ROT // 2, first_i)
    k32 = jnp.where(in_key, kr, 0.0)
    k64 = k32 + pltpu.roll(k32, IDX_DIM, 1)
    ki_ref[0] = (k64 + pltpu.roll(k64, 2 * IDX_DIM, 1)).astype(BF16)
    wi_ref[0] = pltpu.roll(ikw, LANES - IDX_DIM, 1) * (IDX_HEADS ** -0.5 * IDX_DIM ** -0.5)

    cm, sm = cm_ref[0], sm_ref[0]
    first_m = (lane >= MLA_NOPE) & (lane < MLA_NOPE + MLA_ROPE // 2)
    qn = _rmsnorm(mq_ref[...], qn_ref[...], MLA_Q_RANK).astype(BF16)
    q = _dot(qn, wuq_ref[...])
    for h in range(MLA_HEADS):
        sl = slice(h * LANES, (h + 1) * LANES)
        qm_ref[0, :, sl] = _rope128(q[:, sl], cm, sm, MLA_ROPE // 2, first_m).astype(BF16)
    c_kv = _rmsnorm(mkv_ref[...], kvn_ref[...], MLA_KV_RANK).astype(BF16)
    kn = _dot(c_kv, wuk_ref[...])
    vm_ref[0] = _dot(c_kv, wuv_ref[...]).astype(BF16)
    cmk = pltpu.roll(cm, LANES - MLA_NOPE, 1)
    smk = pltpu.roll(sm, LANES - MLA_NOPE, 1)
    krope = _rope128(mkr_ref[...], cmk, smk, MLA_ROPE // 2, lane < MLA_ROPE // 2)
    krope = pltpu.roll(jnp.where(lane < MLA_ROPE, krope, 0.0), MLA_NOPE, 1)
    for h in range(MLA_HEADS):
        sl = slice(h * LANES, (h + 1) * LANES)
        km_ref[0, :, sl] = (kn[:, sl] + krope).astype(BF16)


def _prep(proj, tables, bsz, seq, ts, qn, kvn, wuq, wuk, wuv):
    ns = seq // ts

    def cur(width, col):
        return pl.BlockSpec((ts, width), lambda b, i: (b * ns + i, col // width))

    def tab():
        return pl.BlockSpec((1, ts, LANES), lambda b, i: (b, i, 0))

    def full(shape):
        return pl.BlockSpec(shape, lambda b, i: (0,) * len(shape))

    def out(width):
        return pl.BlockSpec((1, ts, width), lambda b, i: (b, i, 0))

    widths = (512, 512, 512, 256, LANES, LANES, 1024, 1024, 512)
    dtypes = (BF16, BF16, BF16, BF16, BF16, F32, BF16, BF16, BF16)
    return pl.pallas_call(
        _prep_kernel,
        grid=(bsz, ns),
        in_specs=[cur(512, COL_CQ), cur(512, COL_CK), cur(512, COL_CV), cur(512, COL_MQ),
                  cur(256, COL_MKV), cur(256, COL_IQ), cur(LANES, COL_IKW), cur(LANES, COL_MKR)]
                 + [tab() for _ in range(6)]
                 + [full(qn.shape), full(kvn.shape), full(wuq.shape), full(wuk.shape), full(wuv.shape)],
        out_specs=[out(w) for w in widths],
        out_shape=[jax.ShapeDtypeStruct((bsz, seq, w), dt) for w, dt in zip(widths, dtypes)],
        compiler_params=_cparams(("parallel", "arbitrary")),
        name="prep",
    )(*([proj] * 8), *tables, qn, kvn, wuq, wuk, wuv)


def _softmax_step(s, bias, m, l, acc, v, rc):
    rows, width = s.shape
    ps, ms, ls, als = [], [], [], []
    for r0 in range(0, rows, rc):
        cols = []
        for c0 in range(0, width, LANES):
            sc = s[r0:r0 + rc, c0:c0 + LANES]
            if bias is not None:
                rb = r0 % bias.shape[0]
                sc = sc + bias[rb:rb + rc, c0:c0 + LANES]
            cols.append(sc)
        cmax = functools.reduce(jnp.maximum, cols)
        m_old = m[r0:r0 + rc]
        m_new = jnp.maximum(m_old, jnp.max(cmax, axis=1, keepdims=True))
        alpha = jnp.exp(m_old - m_new)
        pcs = [jnp.exp(sc - m_new) for sc in cols]
        psum = functools.reduce(jnp.add, pcs)
        ls.append(alpha * l[r0:r0 + rc] + jnp.sum(psum, axis=1, keepdims=True))
        ms.append(m_new)
        als.append(alpha)
        ps.append(jnp.concatenate(pcs, axis=1).astype(BF16))
    pv = _dot(jnp.concatenate(ps, axis=0), v)
    return jnp.concatenate(ms, axis=0), jnp.concatenate(ls, axis=0), jnp.concatenate(als, axis=0) * acc + pv


def _dsa_kernel(q_ref, k_ref, v_ref, qi_ref, ki_ref, w_ref, o_ref, key_ref, qs_ref, qh_ref, *, tq, tks, n_sel):
    i = pl.program_id(1)
    nkt = ((i + 1) * tq + tks - 1) // tks
    row = i * tq + lax.broadcasted_iota(I32, (tq, 1), 0)
    lane = lax.broadcasted_iota(I32, (1, LANES), 1)

    for h in range(IDX_HEADS):
        qi = qi_ref[0, :, (h // 4) * LANES:(h // 4 + 1) * LANES]
        qs_ref[h * tq:(h + 1) * tq, :] = jnp.where(lane // IDX_DIM == h % 4, qi, jnp.zeros_like(qi))
    for h in range(DSA_HEADS):
        qp = q_ref[0, :, (h // 2) * LANES:(h // 2 + 1) * LANES]
        qh_ref[h // 2, (h % 2) * tq:(h % 2 + 1) * tq, :] = jnp.where(lane // DSA_HEAD_DIM == h % 2, qp,
                                                                     jnp.zeros_like(qp))

    w = w_ref[0]
    w_h = [w[:, h:h + 1] for h in range(IDX_HEADS)]
    hg = 4

    def score_tile(kt, carry):
        k0 = pl.multiple_of(kt * tks, tks)
        kit = ki_ref[0, pl.ds(k0, tks), :]
        acc = jnp.zeros((tq, tks), F32)
        for g in range(IDX_HEADS // hg):
            res = _dot_nt(qs_ref[g * hg * tq:(g + 1) * hg * tq, :], kit)
            for j in range(hg):
                acc = acc + w_h[g * hg + j] * jnp.maximum(res[j * tq:(j + 1) * tq, :], 0.0)
        bits = lax.bitcast_convert_type(acc, I32)
        key = jnp.where(bits >= 0, bits, bits ^ 0x7FFFFFFF)
        col = k0 + lax.broadcasted_iota(I32, (1, tks), 1)
        key_ref[:, pl.ds(k0, tks)] = jnp.where(col <= row, key, INT_MIN)
        return carry

    lax.fori_loop(0, nkt, score_tile, 0)

    def sweep(fn, init):
        def body(kt, part):
            k0 = pl.multiple_of(kt * tks, tks)
            out = []
            for r0 in range(0, tq, 64):
                pr = part[r0:r0 + 64]
                for c in range(tks // LANES):
                    c0 = k0 + c * LANES
                    pr = fn(pr, key_ref[r0:r0 + 64, pl.ds(c0, LANES)], slice(r0, r0 + 64), c0)
                out.append(pr)
            return jnp.concatenate(out, axis=0)
        return lax.fori_loop(0, nkt, body, init)

    def count_ge(cand):
        part = sweep(lambda pr, kk, rs, c0: pr + jnp.where(kk >= cand[rs], 1.0, 0.0), jnp.zeros((tq, LANES), F32))
        return jnp.sum(part, axis=1, keepdims=True)

    zero = jnp.zeros((tq, LANES), I32)
    c_zero = count_ge(zero)
    nonneg = c_zero >= n_sel
    base = jnp.where(nonneg, zero, INT_MIN)
    c_base = jnp.where(nonneg, c_zero, (row + 1).astype(F32))

    def bit_body(j, st):
        base, c_base = st
        cand = base | jnp.left_shift(jnp.int32(1), 30 - j)
        c = count_ge(cand)
        ok = c >= n_sel
        return jnp.where(ok, cand, base), jnp.where(ok, c, c_base)

    thr, c_thr = lax.fori_loop(0, 31, bit_body, (base, c_base))
    thr = jnp.maximum(thr, INT_MIN + 1)

    def fix_ties(rs):
        thr_c = thr[rs]

        def count_if(pred):
            def body(kt, part):
                k0 = pl.multiple_of(kt * tks, tks)
                for c in range(tks // LANES):
                    c0 = k0 + c * LANES
                    part = part + jnp.where(pred(key_ref[rs, pl.ds(c0, LANES)], c0), 1.0, 0.0)
                return part
            part = lax.fori_loop(0, nkt, body, jnp.zeros((64, LANES), F32))
            return jnp.sum(part, axis=1, keepdims=True)

        need = n_sel - count_if(lambda kk, c0: kk > thr_c)

        def col_body(_, st):
            j_lo, j_hi = st
            mid = (j_lo + j_hi) >> 1
            ok = count_if(lambda kk, c0: jnp.logical_and(kk == thr_c, c0 + lane <= mid)) >= need
            return jnp.where(ok, j_lo, mid), jnp.where(ok, mid, j_hi)

        j0 = (jnp.full((64, LANES), -1, I32), jnp.full((64, LANES), 0, I32) + (nkt * tks - 1))
        _, last = lax.fori_loop(0, key_ref.shape[1].bit_length(), col_body, j0)

        def demote(kt, carry):
            k0 = pl.multiple_of(kt * tks, tks)
            for c in range(tks // LANES):
                c0 = k0 + c * LANES
                kk = key_ref[rs, pl.ds(c0, LANES)]
                drop = jnp.logical_and(kk == thr_c, c0 + lane > last)
                key_ref[rs, pl.ds(c0, LANES)] = jnp.where(drop, thr_c - 1, kk)
            return carry

        lax.fori_loop(0, nkt, demote, 0)

    for r0 in range(0, tq, 64):
        rs = slice(r0, r0 + 64)
        pl.when(jnp.max(jnp.where(c_thr[rs] > n_sel, 1, 0)) > 0)(functools.partial(fix_ties, rs))

    def att_tile(kt, carry):
        k0 = pl.multiple_of(kt * tks, tks)
        bias = jnp.concatenate([jnp.where(key_ref[:, pl.ds(k0 + c * LANES, LANES)] >= thr, 0.0, NEG_BIG)
                                for c in range(tks // LANES)], axis=1)
        out = []
        for p in range(DSA_HEADS // 2):
            cols = slice(p * LANES, (p + 1) * LANES)
            m, l, acc = carry[p]
            s = _dot_nt(qh_ref[p], k_ref[0, pl.ds(k0, tks), cols])
            out.append(_softmax_step(s, bias, m, l, acc, v_ref[0, pl.ds(k0, tks), cols], 64))
        return tuple(out)

    init = tuple((jnp.full((2 * tq, LANES), NEG_BIG, F32), jnp.zeros((2 * tq, LANES), F32),
                  jnp.zeros((2 * tq, LANES), F32)) for _ in range(DSA_HEADS // 2))
    fin = lax.fori_loop(0, nkt, att_tile, init)
    for p in range(DSA_HEADS // 2):
        o = fin[p][2] / fin[p][1]
        o_ref[0, :, p * LANES:(p + 1) * LANES] = jnp.where(lane < DSA_HEAD_DIM, o[:tq], o[tq:]).astype(BF16)


def _dsa(qd, kd, vd, qi, ki, wi, tq, tks):
    bsz, seq, _ = qd.shape
    n_sel = min(DSA_TOPK, seq // 4)
    one = pl.Buffered(1)

    def blk(width):
        return pl.BlockSpec((1, tq, width), lambda b, i: (b, i, 0))

    def res(width):
        return pl.BlockSpec((1, seq, width), lambda b, i: (b, 0, 0), pipeline_mode=one)

    return pl.pallas_call(
        functools.partial(_dsa_kernel, tq=tq, tks=tks, n_sel=n_sel),
        grid=(bsz, seq // tq),
        in_specs=[blk(512), res(512), res(512), blk(256), res(LANES), blk(LANES)],
        out_specs=blk(512),
        out_shape=jax.ShapeDtypeStruct((bsz, seq, 512), BF16),
        scratch_shapes=[pltpu.VMEM((tq, seq), I32),
                        pltpu.VMEM((IDX_HEADS * tq, LANES), BF16),
                        pltpu.VMEM((DSA_HEADS // 2, 2 * tq, LANES), BF16)],
        compiler_params=_cparams(("parallel", "arbitrary")),
        name="dsa",
    )(qd, kd, vd, qi, ki, wi)


def _mla_kernel(q_ref, k_ref, v_ref, o_ref, *, tq):
    i = pl.program_id(2)
    scale = (MLA_NOPE + MLA_ROPE) ** -0.5
    lane = lax.broadcasted_iota(I32, (1, LANES), 1)

    def step(k0, carry, diag):
        vt = v_ref[0, pl.ds(k0, tq), :]
        out = []
        for hh in range(2):
            cols = slice(hh * LANES, (hh + 1) * LANES)
            m, l, acc = carry[hh]
            s = _dot_nt(q_ref[0, :, cols], k_ref[0, pl.ds(k0, tq), cols]) * scale
            bias = None
            if diag:
                rowi = lax.broadcasted_iota(I32, (tq, tq), 0)
                coli = lax.broadcasted_iota(I32, (tq, tq), 1)
                bias = jnp.where(coli <= rowi, 0.0, NEG_BIG)
            out.append(_softmax_step(s, bias, m, l, acc, vt, 64))
        return tuple(out)

    init = tuple((jnp.full((tq, LANES), NEG_BIG, F32), jnp.zeros((tq, LANES), F32), jnp.zeros((tq, LANES), F32))
                 for _ in range(2))
    carry = lax.fori_loop(0, i, lambda kt, c: step(pl.multiple_of(kt * tq, tq), c, False), init)
    fin = step(pl.multiple_of(i * tq, tq), carry, True)
    o_ref[0] = jnp.where(lane < MLA_V, fin[0][2] / fin[0][1], fin[1][2] / fin[1][1]).astype(BF16)


def _mla(qm, km, vm, tq):
    bsz, seq, _ = qm.shape
    return pl.pallas_call(
        functools.partial(_mla_kernel, tq=tq),
        grid=(bsz, MLA_HEADS // 2, seq // tq),
        in_specs=[pl.BlockSpec((1, tq, 2 * LANES), lambda b, p, i: (b, i, p)),
                  pl.BlockSpec((1, seq, 2 * LANES), lambda b, p, i: (b, 0, p)),
                  pl.BlockSpec((1, seq, LANES), lambda b, p, i: (b, 0, p))],
        out_specs=pl.BlockSpec((1, tq, LANES), lambda b, p, i: (b, i, p)),
        out_shape=jax.ShapeDtypeStruct((bsz, seq, MLA_HEADS * MLA_V), BF16),
        compiler_params=_cparams(("parallel", "parallel", "arbitrary")),
        name="mla",
    )(qm, km, vm)


def _merge_kernel(a_ref, b_ref, c_ref, d_ref, g_ref, x_ref, wa_ref, wb_ref, wc_ref, wd_ref, wo_ref,
                  lg_ref, lb_ref, h_ref):
    merged = None
    for j, (br, w) in enumerate(((a_ref, wa_ref), (b_ref, wb_ref), (c_ref, wc_ref), (d_ref, wd_ref))):
        y = jax.nn.sigmoid(g_ref[:, j * D_MODEL:(j + 1) * D_MODEL]) * _dot(br[...], w[...])
        merged = y if merged is None else merged + y
    out = _dot(merged.astype(BF16), wo_ref[...])
    h_ref[...] = _layernorm(DEEPNORM_ALPHA * x_ref[...] + out, lg_ref[...], lb_ref[...])


def _merge(ya, yb, yc, yd, proj, x, wa, wb, wc, wd, wo, lg, lb, tm):
    n = x.shape[0]

    def rows(width):
        return pl.BlockSpec((tm, width), lambda i: (i, 0))

    def full(shape):
        return pl.BlockSpec(shape, lambda i: (0,) * len(shape))

    return pl.pallas_call(
        _merge_kernel,
        grid=(n // tm,),
        in_specs=[rows(512), rows(512), rows(512), rows(512), rows(N_BRANCH * D_MODEL), rows(D_MODEL),
                  full(wa.shape), full(wb.shape), full(wc.shape), full(wd.shape), full(wo.shape),
                  full(lg.shape), full(lb.shape)],
        out_specs=rows(D_MODEL),
        out_shape=jax.ShapeDtypeStruct((n, D_MODEL), F32),
        compiler_params=_cparams(("parallel",)),
        name="merge",
    )(ya, yb, yc, yd, proj, x, wa, wb, wc, wd, wo, lg, lb)


def _router_kernel(h_ref, wh_ref, wl_ref, b_ref, eidx_ref, gate_ref, rank_ref, cnt_ref, carry_ref, *, tm):
    i = pl.program_id(0)

    @pl.when(i == 0)
    def _():
        carry_ref[...] = jnp.zeros_like(carry_ref)

    h = h_ref[...]
    hh = h.astype(BF16)
    hl = (h - hh.astype(F32)).astype(BF16)
    wh = wh_ref[...]
    logits = _dot(hh, wh) + _dot(hl, wh) + _dot(hh, wl_ref[...]) + b_ref[...]
    lane = lax.broadcasted_iota(I32, (1, LANES), 1)
    lane_f = lane.astype(F32)
    vals, idxs = [], []
    cur = logits
    for _ in range(TOP_K):
        m = jnp.max(cur, axis=1, keepdims=True)
        ix = jnp.min(jnp.where(cur == m, lane_f, float(LANES)), axis=1, keepdims=True)
        vals.append(m)
        idxs.append(ix)
        cur = jnp.where(lane_f == ix, -jnp.inf, cur)
    ex = [jnp.exp(v - vals[0]) for v in vals]
    den = ex[0] + ex[1] + ex[2] + ex[3]
    hot = [lane_f == ix for ix in idxs]
    cnt_tok = sum(jnp.where(hk, 1.0, 0.0) for hk in hot)
    r_i = lax.broadcasted_iota(I32, (tm, tm), 0)
    c_i = lax.broadcasted_iota(I32, (tm, tm), 1)
    lower = jnp.where(c_i < r_i, 1.0, 0.0).astype(BF16)
    prefix = _dot(lower, cnt_tok.astype(BF16)) + carry_ref[...]
    eidx = jnp.zeros((tm, LANES), F32)
    gate = jnp.zeros((tm, LANES), F32)
    rank = jnp.zeros((tm, LANES), F32)
    for k in range(TOP_K):
        rk = jnp.sum(jnp.where(hot[k], prefix, 0.0), axis=1, keepdims=True)
        eidx = jnp.where(lane == k, idxs[k], eidx)
        gate = jnp.where(lane == k, ex[k] / den, gate)
        rank = jnp.where(lane == k, rk, rank)
    eidx_ref[...] = eidx.astype(I32)
    gate_ref[...] = gate
    rank_ref[...] = rank.astype(I32)
    carry_ref[...] = carry_ref[...] + jnp.sum(cnt_tok, axis=0, keepdims=True)
    cnt_ref[...] = carry_ref[...].astype(I32)


def _router(h, wh, wl, b, tm):
    n = h.shape[0]

    def rows(width):
        return pl.BlockSpec((tm, width), lambda i: (i, 0))

    def full(shape):
        return pl.BlockSpec(shape, lambda i: (0,) * len(shape))

    return pl.pallas_call(
        functools.partial(_router_kernel, tm=tm),
        grid=(n // tm,),
        in_specs=[rows(D_MODEL), full(wh.shape), full(wl.shape), full(b.shape)],
        out_specs=[rows(LANES), rows(LANES), rows(LANES), full((1, LANES))],
        out_shape=[jax.ShapeDtypeStruct((n, LANES), I32), jax.ShapeDtypeStruct((n, LANES), F32),
                   jax.ShapeDtypeStruct((n, LANES), I32), jax.ShapeDtypeStruct((1, LANES), I32)],
        scratch_shapes=[pltpu.VMEM((1, LANES), F32)],
        compiler_params=_cparams(("arbitrary",)),
        name="router",
    )(h, wh, wl, b)


def _row_copy(src_hbm, row, dst, r, sem):
    return pltpu.make_async_copy(src_hbm.at[pl.ds(row, 1), :], dst.at[pl.ds(r, 1), :], sem)


def _idx_copy(idx_hbm, blk, idx_smem, isem):
    return pltpu.make_async_copy(idx_hbm.at[blk, 0], idx_smem, isem)


def _gather_rows(idx_smem, src_hbm, dst, sem, nrows, inline):
    if inline:
        for r in range(nrows):
            _row_copy(src_hbm, idx_smem[r], dst, r, sem).start()
        return

    def body(r, carry):
        _row_copy(src_hbm, idx_smem[r], dst, r, sem).start()
        return carry

    lax.fori_loop(0, nrows, body, 0, unroll=8)


def _wait_rows(src_hbm, dst, sem, nrows):
    pltpu.make_async_copy(src_hbm.at[pl.ds(0, nrows), :], dst, sem).wait()


def _expert_kernel(be_ref, idx_hbm, h_hbm, wgu_ref, bgu_ref, wd_ref, bd_ref, y_ref,
                   xbuf, idx_smem, sem, isem, *, rb, nb):
    del be_ref
    i = pl.program_id(0)
    slot = i % 2

    @pl.when(i == 0)
    def _():
        first = _idx_copy(idx_hbm, 0, idx_smem, isem)
        first.start()
        first.wait()
        _gather_rows(idx_smem, h_hbm, xbuf.at[0], sem.at[0], rb, False)
        _idx_copy(idx_hbm, min(1, nb - 1), idx_smem, isem).start()

    _wait_rows(h_hbm, xbuf.at[slot], sem.at[slot], rb)
    _idx_copy(idx_hbm, 0, idx_smem, isem).wait()
    _gather_rows(idx_smem, h_hbm, xbuf.at[1 - slot], sem.at[1 - slot], rb, True)
    xb = xbuf[slot].astype(BF16)
    fc = 512
    y = None
    for c in range(D_FF // fc):
        g = _dot(xb, wgu_ref[0, :, c * fc:(c + 1) * fc]) + bgu_ref[0, :, c * fc:(c + 1) * fc]
        lin = (_dot(xb, wgu_ref[0, :, D_FF + c * fc:D_FF + (c + 1) * fc])
               + bgu_ref[0, :, D_FF + c * fc:D_FF + (c + 1) * fc])
        g = jnp.minimum(g, SWIGLU_LIMIT)
        lin = jnp.clip(lin, -SWIGLU_LIMIT, SWIGLU_LIMIT)
        act = (g * jax.nn.sigmoid(SWIGLU_ALPHA * g) * (lin + 1.0)).astype(BF16)
        part = _dot(act, wd_ref[0, c * fc:(c + 1) * fc, :])
        y = part if y is None else y + part
    y_ref[...] = y + bd_ref[0]

    @pl.when(i < nb - 1)
    def _():
        _idx_copy(idx_hbm, jnp.minimum(i + 2, nb - 1), idx_smem, isem).start()

    @pl.when(i == nb - 1)
    def _():
        _wait_rows(h_hbm, xbuf.at[1 - slot], sem.at[1 - slot], rb)


def _experts(blk_expert, row_src, h, wgu, bgu, wd, bd, rb):
    nb = blk_expert.shape[0]
    grid_spec = pltpu.PrefetchScalarGridSpec(
        num_scalar_prefetch=1,
        grid=(nb,),
        in_specs=[pl.BlockSpec(memory_space=pl.ANY),
                  pl.BlockSpec(memory_space=pl.ANY),
                  pl.BlockSpec((1, D_MODEL, 2 * D_FF), lambda i, be: (be[i], 0, 0)),
                  pl.BlockSpec((1, 1, 2 * D_FF), lambda i, be: (be[i], 0, 0)),
                  pl.BlockSpec((1, D_FF, D_MODEL), lambda i, be: (be[i], 0, 0)),
                  pl.BlockSpec((1, 1, D_MODEL), lambda i, be: (be[i], 0, 0))],
        out_specs=pl.BlockSpec((rb, D_MODEL), lambda i, be: (i, 0)),
        scratch_shapes=[pltpu.VMEM((2, rb, D_MODEL), F32),
                        pltpu.SMEM((rb,), I32),
                        pltpu.SemaphoreType.DMA((2,)),
                        pltpu.SemaphoreType.DMA(())],
    )
    return pl.pallas_call(
        functools.partial(_expert_kernel, rb=rb, nb=nb),
        grid_spec=grid_spec,
        out_shape=jax.ShapeDtypeStruct((nb * rb, D_MODEL), F32),
        compiler_params=_cparams(("arbitrary",)),
        name="experts",
    )(blk_expert, row_src, h, wgu, bgu, wd, bd)


def _combine_kernel(idx_hbm, y_hbm, gate_ref, h_ref, lg_ref, lb_ref, x_ref, xb_ref,
                    ybuf, idx_smem, sem, isem, *, tm, nt):
    i = pl.program_id(0)
    slot = i % 2

    @pl.when(i == 0)
    def _():
        first = _idx_copy(idx_hbm, 0, idx_smem, isem)
        first.start()
        first.wait()
        _gather_rows(idx_smem, y_hbm, ybuf.at[0], sem.at[0], TOP_K * tm, False)
        _idx_copy(idx_hbm, min(1, nt - 1), idx_smem, isem).start()

    _wait_rows(y_hbm, ybuf.at[slot], sem.at[slot], TOP_K * tm)
    _idx_copy(idx_hbm, 0, idx_smem, isem).wait()
    _gather_rows(idx_smem, y_hbm, ybuf.at[1 - slot], sem.at[1 - slot], TOP_K * tm, True)
    gate = gate_ref[...]
    moe = None
    for k in range(TOP_K):
        t = ybuf[slot, k * tm:(k + 1) * tm, :] * gate[:, k:k + 1]
        moe = t if moe is None else moe + t
    out = _layernorm(DEEPNORM_ALPHA * h_ref[...] + moe, lg_ref[...], lb_ref[...])
    x_ref[...] = out
    xb_ref[...] = out.astype(BF16)

    @pl.when(i < nt - 1)
    def _():
        _idx_copy(idx_hbm, jnp.minimum(i + 2, nt - 1), idx_smem, isem).start()

    @pl.when(i == nt - 1)
    def _():
        _wait_rows(y_hbm, ybuf.at[1 - slot], sem.at[1 - slot], TOP_K * tm)


def _combine(dest_t, ybuf, gate, h, lg, lb, tm):
    n = h.shape[0]
    nt = n // tm

    def rows(width):
        return pl.BlockSpec((tm, width), lambda i: (i, 0))

    def full(shape):
        return pl.BlockSpec(shape, lambda i: (0,) * len(shape))

    return pl.pallas_call(
        functools.partial(_combine_kernel, tm=tm, nt=nt),
        grid=(nt,),
        in_specs=[pl.BlockSpec(memory_space=pl.ANY),
                  pl.BlockSpec(memory_space=pl.ANY),
                  rows(LANES), rows(D_MODEL), full(lg.shape), full(lb.shape)],
        out_specs=[rows(D_MODEL), rows(D_MODEL)],
        out_shape=[jax.ShapeDtypeStruct((n, D_MODEL), F32), jax.ShapeDtypeStruct((n, D_MODEL), BF16)],
        scratch_shapes=[pltpu.VMEM((2, TOP_K * tm, D_MODEL), F32),
                        pltpu.SMEM((TOP_K * tm,), I32),
                        pltpu.SemaphoreType.DMA((2,)),
                        pltpu.SemaphoreType.DMA(())],
        compiler_params=_cparams(("arbitrary",)),
        name="combine",
    )(dest_t, ybuf, gate, h, lg, lb)


def _pad_cols(w, width):
    return jnp.pad(w, ((0, 0), (0, width - w.shape[1])))


def _pack_w_in(w_in):
    offs = np.cumsum((0,) + SPLITS)
    seg = [w_in[:, int(offs[j]):int(offs[j + 1])] for j in range(len(SPLITS))]
    u_pool, u_conv, c_q, c_k, c_v, i_q, i_k, i_w, m_q, m_kv, m_kr, g_all = seg
    parts = [g_all, u_conv, u_pool, c_q, c_k, c_v, _pad_cols(m_q, 512), m_kv, i_q,
             _pad_cols(jnp.concatenate([i_k, i_w], axis=1), LANES), _pad_cols(m_kr, LANES)]
    return jnp.concatenate(parts, axis=1).astype(BF16)


def _pack_heads(w, widths_in, total):
    r = w.shape[0]
    per = sum(widths_in)
    w = w.reshape(r, -1, per)
    w = jnp.pad(w, ((0, 0), (0, 0), (0, total - per)))
    return w.reshape(r, -1)


def _layer(x, xb, tables, bsz, seq, cfg, w_in, pool_w, pool_scale, pool_out, conv_w, conv_b, conv_ln_g,
           conv_ln_b, conv_out, dsa_out, mla_q_norm, mla_kv_norm, mla_wuq, mla_wuk, mla_wuv, mla_out, w_o,
           ln1_g, ln1_b, router_w, router_b, exp_w_gu, exp_b_gu, exp_w_d, exp_b_d, ln2_g, ln2_b):
    n = bsz * seq
    row = lambda v: v.reshape(1, -1).astype(F32)

    proj = _inproj(xb, _pack_w_in(w_in), cfg["tm_in"], cfg["tn_in"])

    ya, yb = _seqmix(proj, bsz, seq, cfg["ts"], pool_w.astype(BF16), row(pool_scale),
                     jnp.pad(conv_w, ((0, HALO - CONV_K), (0, 0))), row(conv_b), row(conv_ln_g), row(conv_ln_b))

    wuq = jnp.pad(_pack_heads(mla_wuq, (MLA_NOPE, MLA_ROPE), LANES), ((0, 512 - MLA_Q_RANK), (0, 0))).astype(BF16)
    wuk = _pack_heads(mla_wuk, (MLA_NOPE,), LANES).astype(BF16)
    qd, kd, vd, qi, ki, wi, qm, km, vm = _prep(
        proj, tables, bsz, seq, cfg["ts"], _pad_cols(row(mla_q_norm), 512), row(mla_kv_norm),
        wuq, wuk, mla_wuv.astype(BF16))

    yc = _dsa(qd, kd, vd, qi, ki, wi, cfg["tq_dsa"], cfg["tk_dsa"]).reshape(n, -1)
    yd = _mla(qm, km, vm, cfg["tq_mla"]).reshape(n, -1)

    h = _merge(ya, yb, yc, yd, proj, x, pool_out.astype(BF16), conv_out.astype(BF16), dsa_out.astype(BF16),
               mla_out.astype(BF16), w_o.astype(BF16), row(ln1_g), row(ln1_b), cfg["tm_merge"])

    rw = _pad_cols(router_w, LANES)
    rwh = rw.astype(BF16)
    rwl = (rw - rwh.astype(F32)).astype(BF16)
    rb_ = jnp.pad(row(router_b), ((0, 0), (0, LANES - N_EXPERTS)), constant_values=NEG_BIG)
    eidx, gate, rank, counts = _router(h, rwh, rwl, rb_, cfg["tm_router"])

    counts = counts[0, :N_EXPERTS]
    padded = (counts + MOE_BLOCK - 1) // MOE_BLOCK * MOE_BLOCK
    pad_end = jnp.cumsum(padded)
    pad_start = pad_end - padded
    n_blocks = -(-(n * TOP_K) // MOE_BLOCK) + N_EXPERTS
    dest = pad_start[eidx[:, :TOP_K]] + rank[:, :TOP_K]
    tok = jnp.broadcast_to(jnp.arange(n, dtype=I32)[:, None], (n, TOP_K))
    row_src = jnp.zeros((n_blocks * MOE_BLOCK,), I32).at[dest.reshape(-1)].set(tok.reshape(-1), unique_indices=True)
    blk_start = jnp.arange(n_blocks, dtype=I32) * MOE_BLOCK
    blk_expert = jnp.minimum(jnp.sum((pad_end[None, :] <= blk_start[:, None]).astype(I32), axis=1), N_EXPERTS - 1)

    ybuf = _experts(blk_expert, row_src.reshape(n_blocks, 1, MOE_BLOCK), h,
                    exp_w_gu.astype(BF16), exp_b_gu.reshape(N_EXPERTS, 1, -1),
                    exp_w_d.astype(BF16), exp_b_d.reshape(N_EXPERTS, 1, -1), MOE_BLOCK)

    tmc = cfg["tm_comb"]
    dest_t = dest.reshape(n // tmc, tmc, TOP_K).transpose(0, 2, 1).reshape(n // tmc, 1, TOP_K * tmc)
    return _combine(dest_t, ybuf, gate, h, row(ln2_g), row(ln2_b), tmc)


_CFG = dict(tm_in=1024, tn_in=768, ts=512, tq_dsa=256, tk_dsa=1024, tq_mla=512, tm_merge=256,
            tm_router=512, tm_comb=128)


def _forward(cfg, x, positions, *weights):
    bsz, seq, d = x.shape
    n = bsz * seq
    tables = (*_rope_tables(positions, DSA_ROT, DSA_HEAD_DIM, 0),
              *_rope_tables(positions, IDX_ROT, IDX_DIM, 0),
              *_rope_tables(positions, MLA_ROPE, LANES, MLA_NOPE))
    xf = x.reshape(n, d)
    xb = xf.astype(BF16)
    for l in range(DEPTH):
        xf, xb = _layer(xf, xb, tables, bsz, seq, cfg, *[w[l] for w in weights])
    return xf.reshape(bsz, seq, d)


def kernel(x, positions, w_in, pool_w, pool_scale, pool_out, conv_w, conv_b, conv_ln_g, conv_ln_b, conv_out,
           dsa_out, mla_q_norm, mla_kv_norm, mla_wuq, mla_wuk, mla_wuv, mla_out, w_o, ln1_g, ln1_b, router_w,
           router_b, exp_w_gu, exp_b_gu, exp_w_d, exp_b_d, ln2_g, ln2_b):
    return _forward(_CFG, x, positions, w_in, pool_w, pool_scale, pool_out, conv_w, conv_b, conv_ln_g,
                    conv_ln_b, conv_out, dsa_out, mla_q_norm, mla_kv_norm, mla_wuq, mla_wuk, mla_wuv, mla_out,
                    w_o, ln1_g, ln1_b, router_w, router_b, exp_w_gu, exp_b_gu, exp_w_d, exp_b_d, ln2_g, ln2_b)
```
